```python
import math
import jax, jax.numpy as jnp
from jax import lax
import numpy as np

D_MODEL = 2048
BATCH = 4
SEQ = 2048
DEPTH = 1
DEC_BATCH = 128
DEC_SEQ = 4
PAST_LEN = 16384
PAGE_SIZE = 128

D_A = D_MODEL
H_A = 8
DH_A = D_A // H_A
D_B = D_MODEL
G_B = 8
CHUNK = 128
CONV_W = 4
EPS = 1e-6

COLUMN_SPLITS = (("q", D_A), ("k", D_A), ("v", D_A), ("o", D_A), ("za", D_A),
                 ("ig", H_A), ("fg", H_A),
                 ("u", D_B), ("vb", D_B), ("zb", D_B),
                 ("ga", D_MODEL), ("gb", D_MODEL))

kernel_name = "hybrid_mlstm_gmlp_gated_decode_step"


def _n_cols():
    return sum(w for _, w in COLUMN_SPLITS)


def _split_columns(p):
    idx = np.cumsum([w for _, w in COLUMN_SPLITS])[:-1].tolist()
    return jnp.split(p, idx, axis=-1)


def _rmsnorm(x, g):
    xf = x.astype(jnp.float32)
    y = xf * lax.rsqrt(jnp.mean(xf * xf, axis=-1, keepdims=True) + EPS) * g.astype(jnp.float32)
    return y.astype(x.dtype)


def _layernorm(x, g, b):
    xf = x.astype(jnp.float32)
    mu = jnp.mean(xf, axis=-1, keepdims=True)
    xc = xf - mu
    y = xc * lax.rsqrt(jnp.mean(xc * xc, axis=-1, keepdims=True) + EPS) * g.astype(jnp.float32) + b.astype(jnp.float32)
    return y.astype(x.dtype)


def _causal_conv(x, buf, w, b):
    T = x.shape[1]
    xp = jnp.concatenate([buf.astype(x.dtype), x], axis=1)
    out = b
    for j in range(CONV_W):
        out = out + w[j] * xp[:, j:j + T]
    return out, xp[:, T:]


def _mlstm_chunkwise(q, k, v, ig, lf, C0, n0, m0):
    B, T, H, D = q.shape
    L = math.gcd(T, CHUNK)
    nc = T // L

    def to_chunks(a):
        return jnp.moveaxis(a.reshape((B, nc, L) + a.shape[2:]), 1, 0)

    xs = tuple(to_chunks(a) for a in (q, k, v, ig, lf))
    causal = jnp.tril(jnp.ones((L, L), dtype=bool))

    def step(carry, inp):
        C, n, m = carry
        qb, kb, vb, ib, fb = inp
        bcum = jnp.cumsum(fb, axis=1)
        dmat = bcum[:, :, None, :] - bcum[:, None, :, :] + ib[:, None, :, :]
        dmat = jnp.where(causal[None, :, :, None], dmat, -jnp.inf)
        inter = bcum + m[:, None, :]
        m_row = jnp.maximum(jnp.max(dmat, axis=2), inter)
        wts = jnp.exp(dmat - m_row[:, :, None, :])
        sc_inter = jnp.exp(inter - m_row)
        s = jnp.einsum('bthd,bshd->btsh', qb, kb) * wts
        num = (jnp.einsum('btsh,bshd->bthd', s, vb)
               + sc_inter[..., None] * jnp.einsum('bhvk,bthk->bthv', C, qb))
        den = jnp.sum(s, axis=2) + sc_inter * jnp.einsum('bhk,bthk->bth', n, qb)
        h = num / jnp.maximum(jnp.abs(den), jnp.exp(-m_row))[..., None]
        b_last = bcum[:, -1]
        src = b_last[:, None, :] - bcum + ib
        m_new = jnp.maximum(b_last + m, jnp.max(src, axis=1))
        ws = jnp.exp(src - m_new[:, None, :])
        decay = jnp.exp(b_last + m - m_new)
        C_new = decay[..., None, None] * C + jnp.einsum('bsh,bshv,bshk->bhvk', ws, vb, kb)
        n_new = decay[..., None] * n + jnp.einsum('bsh,bshk->bhk', ws, kb)
        return (C_new, n_new, m_new), h

    (C, n, m), hs = lax.scan(step, (C0, n0, m0), xs)
    h = jnp.moveaxis(hs, 0, 1).reshape(B, T, H, D)
    return h, C, n, m


def _spatial_gating(u, v, w_s, b_s):
    B, T, _ = v.shape
    L = min(T, CHUNK)
    nc = T // L
    vg = v.reshape(B, nc, L, G_B, D_B // G_B)
    w = w_s[:, :L, :L] * jnp.tril(jnp.ones((L, L), dtype=w_s.dtype))
    s = jnp.einsum('gts,bnsgc->bntgc', w, vg) + b_s[:, :L].T[None, None, :, :, None]
    return u * s.reshape(B, T, D_B)


def _layer(x, conv_buf, C0, n0, m0, g_pre, w_in, b_ig, b_fg, w_conv, b_conv, g_head,
           ln_g, ln_b, w_s, b_s, w_pa, w_pb, w_out, g_post):
    B, T, _ = x.shape
    h = _rmsnorm(x, g_pre)
    proj = jnp.einsum('btd,dn->btn', h, w_in)
    q_pre, k_pre, v_a, o_pre, za, ig_pre, fg_pre, u, vb, zb, ga, gb = _split_columns(proj)

    qk_pre = jnp.concatenate([q_pre, k_pre], axis=-1)
    qk, conv_new = _causal_conv(qk_pre, conv_buf, w_conv, b_conv)
    qk = jax.nn.silu(qk).astype(jnp.float32)
    q = qk[..., :D_A].reshape(B, T, H_A, DH_A)
    k = qk[..., D_A:].reshape(B, T, H_A, DH_A) * (DH_A ** -0.5)
    v = v_a.astype(jnp.float32).reshape(B, T, H_A, DH_A)
    ig = ig_pre.astype(jnp.float32) + b_ig.astype(jnp.float32)
    lf = jax.nn.log_sigmoid(fg_pre.astype(jnp.float32) + b_fg.astype(jnp.float32))
    h_cell, C, n, m = _mlstm_chunkwise(q, k, v, ig, lf, C0.astype(jnp.float32),
                                       n0.astype(jnp.float32), m0.astype(jnp.float32))
    hn = h_cell * lax.rsqrt(jnp.mean(h_cell * h_cell, axis=-1, keepdims=True) + EPS) * g_head.astype(jnp.float32)
    hn = hn.reshape(B, T, D_A).astype(x.dtype)
    y_a = jax.nn.sigmoid(o_pre) * hn * jax.nn.silu(za)

    vb_n = _layernorm(vb, ln_g, ln_b)
    y_b = _spatial_gating(u, vb_n, w_s, b_s) * jax.nn.silu(zb)

    merged = (jax.nn.sigmoid(ga) * jnp.einsum('btc,cd->btd', y_a, w_pa)
              + jax.nn.sigmoid(gb) * jnp.einsum('btc,cd->btd', y_b, w_pb))
    out = jnp.einsum('btd,de->bte', merged, w_out)
    y = x + _rmsnorm(out, g_post)
    start = ((T - 1) // CHUNK) * CHUNK
    vrows = vb_n[:, start:]
    dt = x.dtype
    return y, C.astype(dt), n.astype(dt), m.astype(dt), conv_new.astype(dt), vrows


def setup_inputs(seed: int = 0) -> dict:
    key = jax.random.key(seed)
    ks = jax.random.split(key, 24)
    f32 = jnp.float32
    nrm = lambda k, s, sc: (jax.random.normal(k, s, f32) * sc)
    n_cols = _n_cols()
    return {
        "x_prompt": nrm(ks[0], (BATCH, SEQ, D_MODEL), 1.0),
        "x_sample": nrm(ks[1], (DEC_BATCH, DEC_SEQ, D_MODEL), 1.0),
        "state_mlstm_C": nrm(ks[2], (DEC_BATCH, H_A, DH_A, DH_A), DH_A ** -0.5),
        "state_mlstm_n": nrm(ks[3], (DEC_BATCH, H_A, DH_A), DH_A ** -0.5),
        "state_mlstm_m": nrm(ks[4], (DEC_BATCH, H_A), 1.0),
        "state_conv_qk": nrm(ks[5], (DEC_BATCH, CONV_W - 1, 2 * D_A), 1.0),
        "g_pre": 1.0 + nrm(ks[6], (D_MODEL,), 0.05),
        "w_in": nrm(ks[7], (D_MODEL, n_cols), D_MODEL ** -0.5),
        "b_ig": nrm(ks[8], (H_A,), 0.1),
        "b_fg": jnp.linspace(3.0, 6.0, H_A, dtype=f32) + nrm(ks[9], (H_A,), 0.1),
        "w_conv": nrm(ks[10], (CONV_W, 2 * D_A), CONV_W ** -0.5),
        "b_conv": nrm(ks[11], (2 * D_A,), 0.02),
        "g_head": 1.0 + nrm(ks[12], (H_A, DH_A), 0.05),
        "ln_g": 1.0 + nrm(ks[13], (D_B,), 0.05),
        "ln_b": nrm(ks[14], (D_B,), 0.02),
        "w_s": nrm(ks[15], (G_B, CHUNK, CHUNK), 0.5 * CHUNK ** -0.5),
        "b_s": 1.0 + nrm(ks[16], (G_B, CHUNK), 0.1),
        "w_pa": nrm(ks[17], (D_A, D_MODEL), D_A ** -0.5),
        "w_pb": nrm(ks[18], (D_B, D_MODEL), D_B ** -0.5),
        "w_out": nrm(ks[19], (D_MODEL, D_MODEL), D_MODEL ** -0.5),
        "g_post": 1.0 + nrm(ks[20], (D_MODEL,), 0.05),
    }


def reference(x_prompt, x_sample, state_mlstm_C, state_mlstm_n, state_mlstm_m, state_conv_qk,
              g_pre, w_in, b_ig, b_fg, w_conv, b_conv, g_head, ln_g, ln_b, w_s, b_s,
              w_pa, w_pb, w_out, g_post):
    dt = x_prompt.dtype
    C0 = jnp.zeros((BATCH, H_A, DH_A, DH_A), dt)
    n0 = jnp.zeros((BATCH, H_A, DH_A), dt)
    m0 = jnp.zeros((BATCH, H_A), dt)
    buf0 = jnp.zeros((BATCH, CONV_W - 1, 2 * D_A), dt)
    y_prompt, x_p, n_p, m_p, conv_p, vrows_p = x_prompt, None, None, None, None, None
    y_sample = x_sample
    for _ in range(DEPTH):
        y_prompt, C_p, n_p, m_p, conv_p, vrows_p = _layer(
            y_prompt, buf0, C0, n0, m0, g_pre, w_in, b_ig, b_fg, w_conv, b_conv, g_head,
            ln_g, ln_b, w_s, b_s, w_pa, w_pb, w_out, g_post)
        y_sample, C_s, n_s, m_s, conv_s, vrows_s = _layer(
            y_sample, state_conv_qk, state_mlstm_C, state_mlstm_n, state_mlstm_m,
            g_pre, w_in, b_ig, b_fg, w_conv, b_conv, g_head,
            ln_g, ln_b, w_s, b_s, w_pa, w_pb, w_out, g_post)
    return (y_prompt, y_sample, C_p, n_p, m_p, conv_p, vrows_p, C_s, n_s, m_s, conv_s, vrows_s)
```

```python
import functools

import jax
import jax.numpy as jnp
from jax import lax
from jax.experimental import pallas as pl
from jax.experimental.pallas import tpu as pltpu

F32 = jnp.float32
BF16 = jnp.bfloat16

D_MODEL = 2048
N_HEADS = 8
D_HEAD = 256
N_GROUPS = 8
CONV_W = 4
EPS = 1e-6
BLOCK_ROWS = 128
LANES = 128
N_MAIN = 10 * D_MODEL
GATE_LO = 5 * D_MODEL
COL_Q, COL_K, COL_V, COL_O, COL_ZA, COL_U, COL_VB, COL_ZB, COL_GA, COL_GB = range(10)

NT_DIMS = (((1,), (1,)), ((), ()))
TN_DIMS = (((0,), (0,)), ((), ()))

VMEM_LIMIT_BYTES = 56 * 1024 * 1024


def _sds(shape, dtype):
    return jax.ShapeDtypeStruct(shape, dtype)


def _iota(shape, dim):
    return lax.broadcasted_iota(jnp.int32, shape, dim)


def _dot(a, b):
    return jnp.dot(a, b, preferred_element_type=F32)


def _split_bf16(x):
    hi = x.astype(BF16)
    lo = (x - hi.astype(F32)).astype(BF16)
    return hi, lo


def _mask_dot_left(mask_b, x):
    hi, lo = _split_bf16(x)
    return _dot(mask_b, hi) + _dot(mask_b, lo)


def _mask_dot_right(x, mask_b):
    hi, lo = _split_bf16(x)
    return _dot(hi, mask_b) + _dot(lo, mask_b)


def _log_sigmoid(x):
    return -(jnp.maximum(-x, 0.0) + jnp.log1p(jnp.exp(-jnp.abs(x))))


def _silu(x):
    return x * jax.nn.sigmoid(x)


def _segment_masks(seg_len):
    r = _iota((BLOCK_ROWS, BLOCK_ROWS), 0)
    c = _iota((BLOCK_ROWS, BLOCK_ROWS), 1)
    shift = seg_len.bit_length() - 1
    same = (r >> shift) == (c >> shift)
    causal = jnp.logical_and(same, c <= r)
    causal_t = jnp.logical_and(same, r <= c)
    return same, causal, causal_t


def _as_bf16_mask(m):
    return jnp.where(m, 1.0, 0.0).astype(BF16)


def _proj_kernel(x_ref, g_ref, w_ref, wg_ref, o_ref, gate_ref, h_scr):
    @pl.when(pl.program_id(1) == 0)
    def _():
        x = x_ref[...]
        ms = jnp.mean(x * x, axis=-1, keepdims=True)
        h = (x * lax.rsqrt(ms + EPS) * g_ref[...]).astype(BF16)
        h_scr[...] = h
        gate_ref[...] = _dot(h, wg_ref[...])

    o_ref[...] = _dot(h_scr[...], w_ref[...]).astype(BF16)


def _project(x2d, g_pre, w_main, w_gate, tm, tn):
    rows = x2d.shape[0]
    return pl.pallas_call(
        _proj_kernel,
        grid=(rows // tm, N_MAIN // tn),
        in_specs=[
            pl.BlockSpec((tm, D_MODEL), lambda i, j: (i, 0)),
            pl.BlockSpec((1, D_MODEL), lambda i, j: (0, 0)),
            pl.BlockSpec((D_MODEL, tn), lambda i, j: (0, j)),
            pl.BlockSpec((D_MODEL, LANES), lambda i, j: (0, 0)),
        ],
        out_specs=[
            pl.BlockSpec((tm, tn), lambda i, j: (i, j)),
            pl.BlockSpec((tm, LANES), lambda i, j: (i, 0)),
        ],
        out_shape=[_sds((rows, N_MAIN), BF16), _sds((rows, LANES), F32)],
        scratch_shapes=[pltpu.VMEM((tm, D_MODEL), BF16)],
        compiler_params=pltpu.CompilerParams(
            dimension_semantics=("arbitrary", "arbitrary"),
            vmem_limit_bytes=VMEM_LIMIT_BYTES),
        name="proj",
    )(x2d, g_pre, w_main, w_gate)


def _gate_prep(graw, gbias, causal_b, causal_t_b):
    g = graw + gbias
    bcum = _mask_dot_left(causal_b, _log_sigmoid(g))
    gt = g.T
    bcum_t = _mask_dot_right(_log_sigmoid(gt), causal_t_b)
    return g, bcum, gt, bcum_t


def _conv_taps(x, w, bias, prev_fn):
    acc = bias + w[CONV_W - 1:CONV_W] * x
    for d in range(1, CONV_W):
        xd = prev_fn(d, pltpu.roll(x, d, axis=0))
        acc = acc + w[CONV_W - 1 - d:CONV_W - d] * xd
    return _silu(acc)


def _cell_output(s_qk, vh_b, bt, bs, ig_row, mprev_col, causal, num_inter, den_inter):
    dmat = jnp.where(causal, bt - bs + ig_row, -jnp.inf)
    inter = bt + mprev_col
    m_row = jnp.maximum(jnp.max(dmat, axis=1, keepdims=True), inter)
    wts = jnp.exp(dmat - m_row)
    sc = jnp.exp(inter - m_row)
    s = s_qk * wts
    num = _dot(s.astype(BF16), vh_b) + sc * num_inter
    den = jnp.sum(s, axis=1, keepdims=True) + sc * den_inter
    inv = 1.0 / jnp.maximum(jnp.abs(den), jnp.exp(-m_row))
    return num * inv


def _gated_head_out(hcell, ghead_row, o_b, za_b):
    hn = hcell * lax.rsqrt(jnp.mean(hcell * hcell, axis=-1, keepdims=True) + EPS) * ghead_row
    return (jax.nn.sigmoid(o_b.astype(F32)) * hn * _silu(za_b.astype(F32))).astype(BF16)


def _mlstm_prompt_kernel(qpre_ref, kpre_ref, v_ref, o_ref, za_ref, gate_ref, gbias_ref, buf_ref,
                         c0_ref, n0_ref, m0_ref, wconv_ref, bconv_ref, ghead_ref,
                         ya_ref, c_ref, n_ref, m_ref, tail_scr):
    chunk = pl.program_id(1)

    @pl.when(chunk == 0)
    def _init():
        c_ref[...] = c0_ref[...]
        n_ref[...] = n0_ref[...]
        m_ref[...] = m0_ref[...]
        tail_scr[...] = jnp.zeros_like(tail_scr)
        for d in range(1, CONV_W):
            tail_scr[d - 1, 0:d, :] = buf_ref[CONV_W - 1 - d:CONV_W - 1, :]

    _, causal, causal_t = _segment_masks(BLOCK_ROWS)
    g, bcum, gt, bcum_t = _gate_prep(gate_ref[...], gbias_ref[...],
                                     _as_bf16_mask(causal), _as_bf16_mask(causal_t))
    row8 = _iota((8, D_HEAD), 0)

    for h in range(N_HEADS):
        cols = slice(h * D_HEAD, (h + 1) * D_HEAD)

        def conv_head(pre_ref, off):
            ccols = slice(off + h * D_HEAD, off + (h + 1) * D_HEAD)

            def prev_fn(d, xd):
                top = jnp.where(row8 < d, tail_scr[d - 1, :, ccols], xd[0:8])
                tail_scr[d - 1, :, ccols] = xd[0:8]
                return jnp.concatenate([top, xd[8:]], axis=0)

            return _conv_taps(pre_ref[:, cols].astype(F32), wconv_ref[:, ccols],
                              bconv_ref[:, ccols], prev_fn)

        qh = conv_head(qpre_ref, 0)
        kh = conv_head(kpre_ref, D_MODEL) * (D_HEAD ** -0.5)
        qb = qh.astype(BF16)
        kb = kh.astype(BF16)
        vh_b = v_ref[:, cols]

        bt = bcum[:, N_HEADS + h:N_HEADS + h + 1]
        bs = bcum_t[N_HEADS + h:N_HEADS + h + 1, :]
        ig_col = g[:, h:h + 1]
        ig_row = gt[h:h + 1, :]
        mprev = m_ref[h:h + 1, 0:1]
        c_old = c_ref[h]
        n_old = n_ref[h:h + 1, :]

        s_qk = lax.dot_general(qb, kb, NT_DIMS, preferred_element_type=F32)
        num_inter = lax.dot_general(qb, c_old.astype(BF16), NT_DIMS, preferred_element_type=F32)
        den_inter = jnp.sum(qh * n_old, axis=1, keepdims=True)
        hcell = _cell_output(s_qk, vh_b, bt, bs, ig_row, mprev, causal, num_inter, den_inter)
        ya_ref[:, cols] = _gated_head_out(hcell, ghead_ref[:, cols], o_ref[:, cols], za_ref[:, cols])

        b_last = bt[BLOCK_ROWS - 1:BLOCK_ROWS, :]
        src_row = b_last - bs + ig_row
        m_new = jnp.maximum(b_last + mprev, jnp.max(src_row, axis=1, keepdims=True))
        ws_col = jnp.exp(b_last - bt + ig_col - m_new)
        decay = jnp.exp(b_last + mprev - m_new)
        vw = (vh_b.astype(F32) * ws_col).astype(BF16)
        c_ref[h] = decay * c_old + lax.dot_general(vw, kb, TN_DIMS, preferred_element_type=F32)
        n_ref[h:h + 1, :] = decay * n_old + jnp.sum(ws_col * kh, axis=0, keepdims=True)
        m_ref[h:h + 1, :] = jnp.broadcast_to(m_new, (1, LANES))


def _mlstm_prompt(proj3d, gates3d, gbias, buf0, c0, n0, m0b, w_conv, b_conv, g_head_row):
    batch, seq, _ = proj3d.shape
    nchunk = seq // BLOCK_ROWS

    def stream(kind):
        return pl.BlockSpec((None, BLOCK_ROWS, D_MODEL), lambda b, c: (b, c, kind))

    def per_batch(shape):
        nd = len(shape)
        return pl.BlockSpec((None,) + shape, lambda b, c: (b,) + (0,) * nd)

    def const(shape):
        nd = len(shape)
        return pl.BlockSpec(shape, lambda b, c: (0,) * nd)

    return pl.pallas_call(
        _mlstm_prompt_kernel,
        grid=(batch, nchunk),
        in_specs=[
            stream(COL_Q), stream(COL_K), stream(COL_V), stream(COL_O), stream(COL_ZA),
            pl.BlockSpec((None, BLOCK_ROWS, LANES), lambda b, c: (b, c, 0)),
            const((1, LANES)),
            per_batch((CONV_W - 1, 2 * D_MODEL)),
            per_batch((N_HEADS, D_HEAD, D_HEAD)),
            per_batch((N_HEADS, D_HEAD)),
            per_batch((N_HEADS, LANES)),
            const((CONV_W, 2 * D_MODEL)),
            const((1, 2 * D_MODEL)),
            const((1, D_MODEL)),
        ],
        out_specs=[
            pl.BlockSpec((None, BLOCK_ROWS, D_MODEL), lambda b, c: (b, c, 0)),
            per_batch((N_HEADS, D_HEAD, D_HEAD)),
            per_batch((N_HEADS, D_HEAD)),
            per_batch((N_HEADS, LANES)),
        ],
        out_shape=[
            _sds((batch, seq, D_MODEL), BF16),
            _sds((batch, N_HEADS, D_HEAD, D_HEAD), F32),
            _sds((batch, N_HEADS, D_HEAD), F32),
            _sds((batch, N_HEADS, LANES), F32),
        ],
        scratch_shapes=[pltpu.VMEM((CONV_W - 1, 8, 2 * D_MODEL), F32)],
        compiler_params=pltpu.CompilerParams(
            dimension_semantics=("arbitrary", "arbitrary"),
            vmem_limit_bytes=VMEM_LIMIT_BYTES),
        name="mlstm_prompt",
    )(proj3d, proj3d, proj3d, proj3d, proj3d, gates3d, gbias, buf0, c0, n0, m0b,
      w_conv, b_conv, g_head_row)


def _mlstm_decode_kernel(qpre_ref, kpre_ref, v_ref, o_ref, za_ref, gate_ref, gbias_ref,
                         histq_ref, histk_ref, c0_ref, n0_ref, m0_ref,
                         wq_ref, wk_ref, bq_ref, bk_ref, ghead_ref,
                         ya_ref, c_ref, n_ref, m_ref, rows_scr, dec_scr, *, seg_len):
    h = pl.program_id(1)
    nseg = BLOCK_ROWS // seg_len
    seg_shift = seg_len.bit_length() - 1
    same, causal, causal_t = _segment_masks(seg_len)
    same_b = _as_bf16_mask(same)
    g, bcum, gt, bcum_t = _gate_prep(gate_ref[...], gbias_ref[...],
                                     _as_bf16_mask(causal), _as_bf16_mask(causal_t))
    lf = _log_sigmoid(g)
    btot = _mask_dot_left(same_b, lf)
    btot_t = _mask_dot_right(_log_sigmoid(gt), same_b)
    rows_scr[0] = gt
    rows_scr[1] = bcum_t
    rows_scr[2] = btot_t
    ig_row = rows_scr[0, pl.ds(h, 1), :]
    bs = rows_scr[1, pl.ds(N_HEADS + h, 1), :]
    blast_row = rows_scr[2, pl.ds(N_HEADS + h, 1), :]

    lane = _iota((BLOCK_ROWS, LANES), 1)

    def pick_lane(x, idx):
        return jnp.sum(jnp.where(lane == idx, x, 0.0), axis=1, keepdims=True)

    ig_col = pick_lane(g, h)
    bt = pick_lane(bcum, N_HEADS + h)
    blast_col = pick_lane(btot, N_HEADS + h)
    m0_col = m0_ref[:, 0:1]

    t_idx = _iota((BLOCK_ROWS, D_HEAD), 0) & (seg_len - 1)

    def conv_head(pre_ref, hist_ref, w_ref, b_ref):
        hist = hist_ref[...]

        def prev_fn(d, xd):
            back = (BLOCK_ROWS - (CONV_W - 1 - d)) % BLOCK_ROWS
            hd = hist if back == 0 else pltpu.roll(hist, back, axis=0)
            return jnp.where(t_idx >= d, xd, hd)

        return _conv_taps(pre_ref[...].astype(F32), w_ref[...], b_ref[...], prev_fn)

    qh = conv_head(qpre_ref, histq_ref, wq_ref, bq_ref)
    kh = conv_head(kpre_ref, histk_ref, wk_ref, bk_ref) * (D_HEAD ** -0.5)
    qb = qh.astype(BF16)
    kb = kh.astype(BF16)
    vh_b = v_ref[...]

    src_col = blast_col - bt + ig_col
    src_row = blast_row - bs + ig_row
    seg_max = jnp.max(jnp.where(same, src_row, -jnp.inf), axis=1, keepdims=True)
    m_new_col = jnp.maximum(blast_col + m0_col, seg_max)
    ws_col = jnp.exp(src_col - m_new_col)
    decay_col = jnp.exp(blast_col + m0_col - m_new_col)
    dec_scr[...] = jnp.broadcast_to(decay_col, (BLOCK_ROWS, LANES))
    vwt = (vh_b.astype(F32) * ws_col).T

    row_seg = _iota((BLOCK_ROWS, D_HEAD), 0) >> seg_shift
    lane_seg = _iota((D_HEAD, BLOCK_ROWS), 1) >> seg_shift

    def per_sequence(b, acc):
        c_old = c0_ref[b, 0]
        q_b = jnp.where(row_seg == b, qh, 0.0).astype(BF16)
        acc = acc + lax.dot_general(q_b, c_old.astype(BF16), NT_DIMS, preferred_element_type=F32)
        vw_b = jnp.where(lane_seg == b, vwt, 0.0).astype(BF16)
        dec = dec_scr[pl.ds(b * seg_len, 1), 0:1]
        c_ref[b, 0] = dec * c_old + _dot(vw_b, kb)
        return acc

    num_inter = lax.fori_loop(0, nseg, per_sequence, jnp.zeros((BLOCK_ROWS, D_HEAD), F32))

    n_old = n0_ref[...]
    den_inter = jnp.sum(qh * n_old, axis=1, keepdims=True)
    s_qk = lax.dot_general(qb, kb, NT_DIMS, preferred_element_type=F32)
    hcell = _cell_output(s_qk, vh_b, bt, bs, ig_row, m0_col, causal, num_inter, den_inter)
    ya_ref[...] = _gated_head_out(hcell, ghead_ref[...], o_ref[...], za_ref[...])

    n_ref[...] = decay_col * n_old + _mask_dot_left(same_b, ws_col * kh)
    m_ref[...] = jnp.broadcast_to(m_new_col, (BLOCK_ROWS, LANES))


def _mlstm_decode(proj2d, gates2d, gbias, hist, c0, n0_rows, m0_rows, w_conv, b_conv, g_head_row,
                  seg_len):
    rows = proj2d.shape[0]
    nblk = rows // BLOCK_ROWS
    nseg = BLOCK_ROWS // seg_len

    def stream(kind):
        return pl.BlockSpec((BLOCK_ROWS, D_HEAD), lambda i, h: (i, kind * N_HEADS + h))

    def head_cols(nrows, off):
        return pl.BlockSpec((nrows, D_HEAD), lambda i, h: (0, off * N_HEADS + h))

    row_head = pl.BlockSpec((BLOCK_ROWS, D_HEAD), lambda i, h: (i, h))
    state = pl.BlockSpec((nseg, 1, D_HEAD, D_HEAD), lambda i, h: (i, h, 0, 0))
    m_rows = pl.BlockSpec((None, BLOCK_ROWS, LANES), lambda i, h: (h, i, 0))

    return pl.pallas_call(
        functools.partial(_mlstm_decode_kernel, seg_len=seg_len),
        grid=(nblk, N_HEADS),
        in_specs=[
            stream(COL_Q), stream(COL_K), stream(COL_V), stream(COL_O), stream(COL_ZA),
            pl.BlockSpec((BLOCK_ROWS, LANES), lambda i, h: (i, 0)),
            pl.BlockSpec((1, LANES), lambda i, h: (0, 0)),
            pl.BlockSpec((BLOCK_ROWS, D_HEAD), lambda i, h: (i, h)),
            pl.BlockSpec((BLOCK_ROWS, D_HEAD), lambda i, h: (i, N_HEADS + h)),
            state, row_head, m_rows,
            head_cols(CONV_W, 0), head_cols(CONV_W, 1), head_cols(1, 0), head_cols(1, 1),
            head_cols(1, 0),
        ],
        out_specs=[row_head, state, row_head, m_rows],
        out_shape=[
            _sds((rows, D_MODEL), BF16),
            _sds(c0.shape, F32),
            _sds((rows, D_MODEL), F32),
            _sds((N_HEADS, rows, LANES), F32),
        ],
        scratch_shapes=[pltpu.VMEM((3, BLOCK_ROWS, BLOCK_ROWS), F32),
                        pltpu.VMEM((BLOCK_ROWS, LANES), F32)],
        compiler_params=pltpu.CompilerParams(
            dimension_semantics=("arbitrary", "arbitrary"),
            vmem_limit_bytes=VMEM_LIMIT_BYTES),
        name="mlstm_decode",
    )(proj2d, proj2d, proj2d, proj2d, proj2d, gates2d, gbias, hist, hist, c0, n0_rows, m0_rows,
      w_conv, w_conv, b_conv, b_conv, g_head_row)


def _merge_kernel(ya_ref, u_ref, vb_ref, zb_ref, ga_ref, gb_ref, x_ref, lng_ref, lnb_ref,
                  ws_ref, bcol_ref, wpa_ref, wpb_ref, wout_ref, gpost_ref,
                  y_ref, vrows_ref, yb_scr, *, seg_len, vrows_every, tm):
    _, causal, _ = _segment_masks(seg_len)
    vb = vb_ref[...].astype(F32)
    mu = jnp.mean(vb, axis=-1, keepdims=True)
    xc = vb - mu
    vbn = xc * lax.rsqrt(jnp.mean(xc * xc, axis=-1, keepdims=True) + EPS) * lng_ref[...] + lnb_ref[...]
    vbn_b = vbn.astype(BF16)

    gcols = D_MODEL // N_GROUPS
    for grp in range(N_GROUPS):
        w_g = jnp.where(causal, ws_ref[grp], 0.0).astype(BF16)
        bias = bcol_ref[:, grp:grp + 1]
        cols = slice(grp * gcols, (grp + 1) * gcols)
        for blk in range(tm // BLOCK_ROWS):
            rows = slice(blk * BLOCK_ROWS, (blk + 1) * BLOCK_ROWS)
            s = _dot(w_g, vbn_b[rows, cols]) + bias
            yb = u_ref[rows, cols].astype(F32) * s * _silu(zb_ref[rows, cols].astype(F32))
            yb_scr[rows, cols] = yb.astype(BF16)

    pa = _dot(ya_ref[...], wpa_ref[...])
    pb = _dot(yb_scr[...], wpb_ref[...])
    merged = (jax.nn.sigmoid(ga_ref[...].astype(F32)) * pa
              + jax.nn.sigmoid(gb_ref[...].astype(F32)) * pb)
    out = _dot(merged.astype(BF16), wout_ref[...])
    post = out * lax.rsqrt(jnp.mean(out * out, axis=-1, keepdims=True) + EPS) * gpost_ref[...]
    y_ref[...] = x_ref[...] + post

    if vrows_every == 1:
        vrows_ref[...] = vbn
    else:
        @pl.when(pl.program_id(0) % vrows_every == vrows_every - 1)
        def _():
            vrows_ref[...] = vbn[tm - BLOCK_ROWS:, :]


def _merge(ya2d, proj2d, x2d, ln_g, ln_b, w_eff, bcol, w_pa, w_pb, w_out, g_post,
           seg_len, rows_per_seq, tm):
    rows = x2d.shape[0]
    if rows_per_seq:
        vrows_every = rows_per_seq // tm
        vrows_shape = (rows // rows_per_seq * BLOCK_ROWS, D_MODEL)
        vrows_spec = pl.BlockSpec((BLOCK_ROWS, D_MODEL), lambda i: (i // vrows_every, 0))
    else:
        vrows_every = 1
        vrows_shape = (rows, D_MODEL)
        vrows_spec = pl.BlockSpec((tm, D_MODEL), lambda i: (i, 0))

    def stream(kind):
        return pl.BlockSpec((tm, D_MODEL), lambda i: (i, kind))

    def const(shape, single=False):
        nd = len(shape)
        if single:
            return pl.BlockSpec(shape, lambda i: (0,) * nd, pipeline_mode=pl.Buffered(1))
        return pl.BlockSpec(shape, lambda i: (0,) * nd)

    weight = const((D_MODEL, D_MODEL), single=True)
    return pl.pallas_call(
        functools.partial(_merge_kernel, seg_len=seg_len, vrows_every=vrows_every, tm=tm),
        grid=(rows // tm,),
        in_specs=[
            pl.BlockSpec((tm, D_MODEL), lambda i: (i, 0)),
            stream(COL_U), stream(COL_VB), stream(COL_ZB), stream(COL_GA), stream(COL_GB),
            pl.BlockSpec((tm, D_MODEL), lambda i: (i, 0)),
            const((1, D_MODEL)), const((1, D_MODEL)),
            const((N_GROUPS, BLOCK_ROWS, BLOCK_ROWS)), const((BLOCK_ROWS, LANES)),
            weight, weight, weight,
            const((1, D_MODEL)),
        ],
        out_specs=[pl.BlockSpec((tm, D_MODEL), lambda i: (i, 0)), vrows_spec],
        out_shape=[_sds((rows, D_MODEL), F32), _sds(vrows_shape, F32)],
        scratch_shapes=[pltpu.VMEM((tm, D_MODEL), BF16)],
        compiler_params=pltpu.CompilerParams(
            dimension_semantics=("arbitrary",),
            vmem_limit_bytes=VMEM_LIMIT_BYTES),
        name="merge",
    )(ya2d, proj2d, proj2d, proj2d, proj2d, proj2d, x2d, ln_g, ln_b, w_eff, bcol,
      w_pa, w_pb, w_out, g_post)


def _pad_lanes(a):
    return jnp.pad(a, ((0, 0), (0, LANES - a.shape[1])))


def kernel(x_prompt, x_sample, state_mlstm_C, state_mlstm_n, state_mlstm_m, state_conv_qk,
           g_pre, w_in, b_ig, b_fg, w_conv, b_conv, g_head, ln_g, ln_b, w_s, b_s,
           w_pa, w_pb, w_out, g_post):
    batch, seq, _ = x_prompt.shape
    dec_batch, dec_seq, _ = x_sample.shape
    dec_rows = dec_batch * dec_seq
    assert seq % BLOCK_ROWS == 0 and BLOCK_ROWS % dec_seq == 0 and dec_rows % BLOCK_ROWS == 0
    assert dec_seq >= CONV_W - 1 and dec_seq & (dec_seq - 1) == 0

    w_main = jnp.concatenate([w_in[:, :GATE_LO], w_in[:, GATE_LO + 2 * N_HEADS:]], axis=1).astype(BF16)
    w_gate = _pad_lanes(w_in[:, GATE_LO:GATE_LO + 2 * N_HEADS]).astype(BF16)
    gbias = _pad_lanes(jnp.concatenate([b_ig, b_fg])[None, :])
    g_pre2 = g_pre[None, :]
    b_conv2 = b_conv[None, :]
    g_head_row = g_head.reshape(1, D_MODEL)
    ln_g2, ln_b2, g_post2 = ln_g[None, :], ln_b[None, :], g_post[None, :]
    w_pa_b, w_pb_b, w_out_b = w_pa.astype(BF16), w_pb.astype(BF16), w_out.astype(BF16)

    xp2d = x_prompt.reshape(batch * seq, D_MODEL)
    proj_p, gates_p = _project(xp2d, g_pre2, w_main, w_gate, tm=1024, tn=1024)
    proj_p3 = proj_p.reshape(batch, seq, N_MAIN)
    zeros = functools.partial(jnp.zeros, dtype=F32)
    ya_p, c_p, n_p, m_pb = _mlstm_prompt(
        proj_p3, gates_p.reshape(batch, seq, LANES), gbias,
        zeros((batch, CONV_W - 1, 2 * D_MODEL)), zeros((batch, N_HEADS, D_HEAD, D_HEAD)),
        zeros((batch, N_HEADS, D_HEAD)), zeros((batch, N_HEADS, LANES)),
        w_conv, b_conv2, g_head_row)
    y_p, vrows_p = _merge(ya_p.reshape(batch * seq, D_MODEL), proj_p, xp2d, ln_g2, ln_b2,
                          w_s, _pad_lanes(b_s.T), w_pa_b, w_pb_b, w_out_b, g_post2,
                          seg_len=BLOCK_ROWS, rows_per_seq=seq, tm=256)
    conv_p = proj_p3[:, seq - (CONV_W - 1):, :2 * D_MODEL].astype(F32)

    xs2d = x_sample.reshape(dec_rows, D_MODEL)
    proj_s, gates_s = _project(xs2d, g_pre2, w_main, w_gate, tm=dec_rows, tn=1024)
    hist = jnp.pad(state_conv_qk, ((0, 0), (0, dec_seq - (CONV_W - 1)), (0, 0)))
    hist = hist.reshape(dec_rows, 2 * D_MODEL)
    n0_rows = jnp.repeat(state_mlstm_n.reshape(dec_batch, D_MODEL), dec_seq, axis=0)
    m0_rows = jnp.broadcast_to(jnp.repeat(state_mlstm_m.T, dec_seq, axis=1)[:, :, None],
                               (N_HEADS, dec_rows, LANES))
    ya_s, c_s, n_rows, m_rows = _mlstm_decode(
        proj_s, gates_s, gbias, hist, state_mlstm_C, n0_rows, m0_rows, w_conv, b_conv2, g_head_row,
        seg_len=dec_seq)
    nrep = BLOCK_ROWS // dec_seq
    w_eff_s = jnp.tile(w_s[:, :dec_seq, :dec_seq], (1, nrep, nrep))
    bcol_s = _pad_lanes(jnp.tile(b_s[:, :dec_seq], (1, nrep)).T)
    y_s, vrows_s = _merge(ya_s, proj_s, xs2d, ln_g2, ln_b2, w_eff_s, bcol_s,
                          w_pa_b, w_pb_b, w_out_b, g_post2,
                          seg_len=dec_seq, rows_per_seq=0, tm=256)
    n_s = n_rows[::dec_seq].reshape(dec_batch, N_HEADS, D_HEAD)
    m_s = m_rows[:, ::dec_seq, 0].T
    conv_s = proj_s[:, :2 * D_MODEL].astype(F32).reshape(dec_batch, dec_seq, 2 * D_MODEL)
    conv_s = conv_s[:, dec_seq - (CONV_W - 1):]

    return (y_p.reshape(batch, seq, D_MODEL), y_s.reshape(dec_batch, dec_seq, D_MODEL),
            c_p, n_p, m_pb[:, :, 0], conv_p, vrows_p.reshape(batch, BLOCK_ROWS, D_MODEL),
            c_s, n_s, m_s, conv_s, vrows_s.reshape(dec_batch, dec_seq, D_MODEL))
```

```python
import functools

import jax
import jax.numpy as jnp
from jax import lax
from jax.experimental import pallas as pl
from jax.experimental.pallas import tpu as pltpu

F32 = jnp.float32
BF16 = jnp.bfloat16

D_MODEL = 2048
N_HEADS = 8
D_HEAD = 256
N_GROUPS = 8
CONV_W = 4
EPS = 1e-6
BLOCK_ROWS = 128
LANES = 128
N_MAIN = 10 * D_MODEL
GATE_LO = 5 * D_MODEL
N_GATE_COLS = 2 * N_HEADS
NORM_ROWS = 512
W_CAST_ROWS = 256
COL_Q, COL_K, COL_V, COL_O, COL_ZA, COL_U, COL_VB, COL_ZB, COL_GA, COL_GB = range(10)

NT_DIMS = (((1,), (1,)), ((), ()))
TN_DIMS = (((0,), (0,)), ((), ()))

VMEM_LIMIT_BYTES = 56 * 1024 * 1024


def _sds(shape, dtype):
    return jax.ShapeDtypeStruct(shape, dtype)


def _iota(shape, dim):
    return lax.broadcasted_iota(jnp.int32, shape, dim)


def _dot(a, b):
    return jnp.dot(a, b, preferred_element_type=F32)


def _split_bf16(x):
    hi = x.astype(BF16)
    lo = (x - hi.astype(F32)).astype(BF16)
    return hi, lo


def _mask_dot_left(mask_b, x):
    hi, lo = _split_bf16(x)
    return _dot(mask_b, hi) + _dot(mask_b, lo)


def _mask_dot_right(x, mask_b):
    hi, lo = _split_bf16(x)
    return _dot(hi, mask_b) + _dot(lo, mask_b)


def _log_sigmoid(x):
    return -(jnp.maximum(-x, 0.0) + jnp.log1p(jnp.exp(-jnp.abs(x))))


def _silu(x):
    return x * jax.nn.sigmoid(x)


def _segment_masks(seg_len):
    r = _iota((BLOCK_ROWS, BLOCK_ROWS), 0)
    c = _iota((BLOCK_ROWS, BLOCK_ROWS), 1)
    shift = seg_len.bit_length() - 1
    same = (r >> shift) == (c >> shift)
    causal = jnp.logical_and(same, c <= r)
    causal_t = jnp.logical_and(same, r <= c)
    return same, causal, causal_t


def _as_bf16_mask(m):
    return jnp.where(m, 1.0, 0.0).astype(BF16)


def _norm_kernel(xp_ref, xs_ref, g_ref, wg_ref, h_ref, gate_ref, *, n_prompt_blocks):
    def emit(x):
        ms = jnp.mean(x * x, axis=-1, keepdims=True)
        h = (x * lax.rsqrt(ms + EPS) * g_ref[...]).astype(BF16)
        h_ref[...] = h
        gate_ref[...] = _dot(h, wg_ref[...].astype(BF16))

    i = pl.program_id(0)

    @pl.when(i < n_prompt_blocks)
    def _():
        emit(xp_ref[...])

    @pl.when(i >= n_prompt_blocks)
    def _():
        emit(xs_ref[...])


def _normalize(xp2d, xs2d, g_pre, w_in):
    tm = NORM_ROWS
    n_prompt_blocks = xp2d.shape[0] // tm
    nblk = n_prompt_blocks + xs2d.shape[0] // tm
    rows = nblk * tm
    return pl.pallas_call(
        functools.partial(_norm_kernel, n_prompt_blocks=n_prompt_blocks),
        grid=(nblk,),
        in_specs=[
            pl.BlockSpec((tm, D_MODEL), lambda i: (jnp.minimum(i, n_prompt_blocks - 1), 0)),
            pl.BlockSpec((tm, D_MODEL), lambda i: (jnp.maximum(i - n_prompt_blocks, 0), 0)),
            pl.BlockSpec((1, D_MODEL), lambda i: (0, 0)),
            pl.BlockSpec((D_MODEL, LANES), lambda i: (0, GATE_LO // LANES)),
        ],
        out_specs=[
            pl.BlockSpec((tm, D_MODEL), lambda i: (i, 0)),
            pl.BlockSpec((tm, LANES), lambda i: (i, 0)),
        ],
        out_shape=[_sds((rows, D_MODEL), BF16), _sds((rows, LANES), F32)],
        compiler_params=pltpu.CompilerParams(
            dimension_semantics=("arbitrary",), vmem_limit_bytes=VMEM_LIMIT_BYTES),
        name="norm",
    )(xp2d, xs2d, g_pre, w_in)


def _proj_kernel(h_ref, wa_ref, wb_ref, o_ref, w_scr, *, tn, first_shifted):
    j = pl.program_id(0)
    nchunks = D_MODEL // W_CAST_ROWS

    def rows_of(r):
        return pl.ds(pl.multiple_of(r * W_CAST_ROWS, W_CAST_ROWS), W_CAST_ROWS)

    @pl.when(pl.program_id(1) == 0)
    def _cast():
        @pl.when(j < first_shifted)
        def _():
            def body(r, carry):
                w_scr[rows_of(r), :] = wa_ref[rows_of(r), :].astype(BF16)
                return carry
            lax.fori_loop(0, nchunks, body, 0)

        @pl.when(j >= first_shifted)
        def _():
            lane = _iota((W_CAST_ROWS, LANES), 1)

            def body(r, carry):
                a = pltpu.roll(wa_ref[rows_of(r), :], tn - N_GATE_COLS, axis=1)
                b = pltpu.roll(wb_ref[rows_of(r), :], LANES - N_GATE_COLS, axis=1)
                tail = jnp.where(lane >= LANES - N_GATE_COLS, b, a[:, tn - LANES:])
                w_scr[rows_of(r), :tn - LANES] = a[:, :tn - LANES].astype(BF16)
                w_scr[rows_of(r), tn - LANES:] = tail.astype(BF16)
                return carry
            lax.fori_loop(0, nchunks, body, 0)

    o_ref[...] = _dot(h_ref[...], w_scr[...]).astype(BF16)


def _project(h_all, w_in, tm, tn):
    rows = h_all.shape[0]
    return pl.pallas_call(
        functools.partial(_proj_kernel, tn=tn, first_shifted=GATE_LO // tn),
        grid=(N_MAIN // tn, rows // tm),
        in_specs=[
            pl.BlockSpec((tm, D_MODEL), lambda j, i: (i, 0)),
            pl.BlockSpec((D_MODEL, tn), lambda j, i: (0, j)),
            pl.BlockSpec((D_MODEL, LANES), lambda j, i: (0, (j + 1) * (tn // LANES))),
        ],
        out_specs=pl.BlockSpec((tm, tn), lambda j, i: (i, j)),
        out_shape=_sds((rows, N_MAIN), BF16),
        scratch_shapes=[pltpu.VMEM((D_MODEL, tn), BF16)],
        compiler_params=pltpu.CompilerParams(
            dimension_semantics=("arbitrary", "arbitrary"),
            vmem_limit_bytes=VMEM_LIMIT_BYTES),
        name="proj",
    )(h_all, w_in, w_in)


def _gate_prep(graw, gbias, causal_b, causal_t_b):
    g = graw + gbias
    bcum = _mask_dot_left(causal_b, _log_sigmoid(g))
    gt = g.T
    bcum_t = _mask_dot_right(_log_sigmoid(gt), causal_t_b)
    return g, bcum, gt, bcum_t


def _conv_taps(x, w, bias, prev_fn):
    acc = bias + w[CONV_W - 1:CONV_W] * x
    for d in range(1, CONV_W):
        xd = prev_fn(d, pltpu.roll(x, d, axis=0))
        acc = acc + w[CONV_W - 1 - d:CONV_W - d] * xd
    return _silu(acc)


def _cell_output(s_qk, vh_b, bt, bs, ig_row, mprev_col, causal, num_inter, den_inter):
    dmat = jnp.where(causal, bt - bs + ig_row, -jnp.inf)
    inter = bt + mprev_col
    m_row = jnp.maximum(jnp.max(dmat, axis=1, keepdims=True), inter)
    wts = jnp.exp(dmat - m_row)
    sc = jnp.exp(inter - m_row)
    s = s_qk * wts
    num = _dot(s.astype(BF16), vh_b) + sc * num_inter
    den = jnp.sum(s, axis=1, keepdims=True) + sc * den_inter
    inv = 1.0 / jnp.maximum(jnp.abs(den), jnp.exp(-m_row))
    return num * inv


def _gated_head_out(hcell, ghead_row, o_b, za_b):
    hn = hcell * lax.rsqrt(jnp.mean(hcell * hcell, axis=-1, keepdims=True) + EPS) * ghead_row
    return (jax.nn.sigmoid(o_b.astype(F32)) * hn * _silu(za_b.astype(F32))).astype(BF16)


def _mlstm_prompt_kernel(qpre_ref, kpre_ref, v_ref, o_ref, za_ref, gate_ref, gbias_ref, buf_ref,
                         c0_ref, n0_ref, m0_ref, wconv_ref, bconv_ref, ghead_ref,
                         ya_ref, c_ref, n_ref, m_ref, tail_scr):
    chunk = pl.program_id(1)

    @pl.when(chunk == 0)
    def _init():
        c_ref[...] = c0_ref[...]
        n_ref[...] = n0_ref[...]
        m_ref[...] = m0_ref[...]
        tail_scr[...] = jnp.zeros_like(tail_scr)
        for d in range(1, CONV_W):
            tail_scr[d - 1, 0:d, :] = buf_ref[CONV_W - 1 - d:CONV_W - 1, :]

    _, causal, causal_t = _segment_masks(BLOCK_ROWS)
    g, bcum, gt, bcum_t = _gate_prep(gate_ref[...], gbias_ref[...],
                                     _as_bf16_mask(causal), _as_bf16_mask(causal_t))
    row8 = _iota((8, D_HEAD), 0)

    for h in range(N_HEADS):
        cols = slice(h * D_HEAD, (h + 1) * D_HEAD)

        def conv_head(pre_ref, off):
            ccols = slice(off + h * D_HEAD, off + (h + 1) * D_HEAD)

            def prev_fn(d, xd):
                top = jnp.where(row8 < d, tail_scr[d - 1, :, ccols], xd[0:8])
                tail_scr[d - 1, :, ccols] = xd[0:8]
                return jnp.concatenate([top, xd[8:]], axis=0)

            return _conv_taps(pre_ref[:, cols].astype(F32), wconv_ref[:, ccols],
                              bconv_ref[:, ccols], prev_fn)

        qh = conv_head(qpre_ref, 0)
        kh = conv_head(kpre_ref, D_MODEL) * (D_HEAD ** -0.5)
        qb = qh.astype(BF16)
        kb = kh.astype(BF16)
        vh_b = v_ref[:, cols]

        bt = bcum[:, N_HEADS + h:N_HEADS + h + 1]
        bs = bcum_t[N_HEADS + h:N_HEADS + h + 1, :]
        ig_col = g[:, h:h + 1]
        ig_row = gt[h:h + 1, :]
        mprev = m_ref[h:h + 1, 0:1]
        c_old = c_ref[h]
        n_old = n_ref[h:h + 1, :]

        s_qk = lax.dot_general(qb, kb, NT_DIMS, preferred_element_type=F32)
        num_inter = lax.dot_general(qb, c_old.astype(BF16), NT_DIMS, preferred_element_type=F32)
        den_inter = jnp.sum(qh * n_old, axis=1, keepdims=True)
        hcell = _cell_output(s_qk, vh_b, bt, bs, ig_row, mprev, causal, num_inter, den_inter)
        ya_ref[:, cols] = _gated_head_out(hcell, ghead_ref[:, cols], o_ref[:, cols], za_ref[:, cols])

        b_last = bt[BLOCK_ROWS - 1:BLOCK_ROWS, :]
        src_row = b_last - bs + ig_row
        m_new = jnp.maximum(b_last + mprev, jnp.max(src_row, axis=1, keepdims=True))
        ws_col = jnp.exp(b_last - bt + ig_col - m_new)
        decay = jnp.exp(b_last + mprev - m_new)
        vw = (vh_b.astype(F32) * ws_col).astype(BF16)
        c_ref[h] = decay * c_old + lax.dot_general(vw, kb, TN_DIMS, preferred_element_type=F32)
        n_ref[h:h + 1, :] = decay * n_old + jnp.sum(ws_col * kh, axis=0, keepdims=True)
        m_ref[h:h + 1, :] = jnp.broadcast_to(m_new, (1, LANES))


def _mlstm_prompt(proj, gates, gbias, buf0, c0, n0, m0b, w_conv, b_conv, g_head_row, batch, seq):
    nchunk = seq // BLOCK_ROWS

    def stream(kind):
        return pl.BlockSpec((BLOCK_ROWS, D_MODEL), lambda b, c: (b * nchunk + c, kind))

    def per_batch(shape):
        nd = len(shape)
        return pl.BlockSpec((None,) + shape, lambda b, c: (b,) + (0,) * nd)

    def const(shape):
        nd = len(shape)
        return pl.BlockSpec(shape, lambda b, c: (0,) * nd)

    return pl.pallas_call(
        _mlstm_prompt_kernel,
        grid=(batch, nchunk),
        in_specs=[
            stream(COL_Q), stream(COL_K), stream(COL_V), stream(COL_O), stream(COL_ZA),
            pl.BlockSpec((BLOCK_ROWS, LANES), lambda b, c: (b * nchunk + c, 0)),
            const((1, LANES)),
            per_batch((CONV_W - 1, 2 * D_MODEL)),
            per_batch((N_HEADS, D_HEAD, D_HEAD)),
            per_batch((N_HEADS, D_HEAD)),
            per_batch((N_HEADS, LANES)),
            const((CONV_W, 2 * D_MODEL)),
            const((1, 2 * D_MODEL)),
            const((1, D_MODEL)),
        ],
        out_specs=[
            pl.BlockSpec((None, BLOCK_ROWS, D_MODEL), lambda b, c: (b, c, 0)),
            per_batch((N_HEADS, D_HEAD, D_HEAD)),
            per_batch((N_HEADS, D_HEAD)),
            per_batch((N_HEADS, LANES)),
        ],
        out_shape=[
            _sds((batch, seq, D_MODEL), BF16),
            _sds((batch, N_HEADS, D_HEAD, D_HEAD), F32),
            _sds((batch, N_HEADS, D_HEAD), F32),
            _sds((batch, N_HEADS, LANES), F32),
        ],
        scratch_shapes=[pltpu.VMEM((CONV_W - 1, 8, 2 * D_MODEL), F32)],
        compiler_params=pltpu.CompilerParams(
            dimension_semantics=("arbitrary", "arbitrary"),
            vmem_limit_bytes=VMEM_LIMIT_BYTES),
        name="mlstm_prompt",
    )(proj, proj, proj, proj, proj, gates, gbias, buf0, c0, n0, m0b,
      w_conv, b_conv, g_head_row)


def _mlstm_decode_kernel(qpre_ref, kpre_ref, v_ref, o_ref, za_ref, gate_ref, gbias_ref,
                         histq_ref, histk_ref, c0_ref, n0_ref, m0_ref,
                         wq_ref, wk_ref, bq_ref, bk_ref, ghead_ref,
                         ya_ref, c_ref, n_ref, m_ref, rows_scr, dec_scr, *, seg_len):
    h = pl.program_id(1)
    nseg = BLOCK_ROWS // seg_len
    seg_shift = seg_len.bit_length() - 1
    same, causal, causal_t = _segment_masks(seg_len)
    same_b = _as_bf16_mask(same)
    g, bcum, gt, bcum_t = _gate_prep(gate_ref[...], gbias_ref[...],
                                     _as_bf16_mask(causal), _as_bf16_mask(causal_t))
    lf = _log_sigmoid(g)
    btot = _mask_dot_left(same_b, lf)
    btot_t = _mask_dot_right(_log_sigmoid(gt), same_b)
    rows_scr[0] = gt
    rows_scr[1] = bcum_t
    rows_scr[2] = btot_t
    ig_row = rows_scr[0, pl.ds(h, 1), :]
    bs = rows_scr[1, pl.ds(N_HEADS + h, 1), :]
    blast_row = rows_scr[2, pl.ds(N_HEADS + h, 1), :]

    lane = _iota((BLOCK_ROWS, LANES), 1)

    def pick_lane(x, idx):
        return jnp.sum(jnp.where(lane == idx, x, 0.0), axis=1, keepdims=True)

    ig_col = pick_lane(g, h)
    bt = pick_lane(bcum, N_HEADS + h)
    blast_col = pick_lane(btot, N_HEADS + h)
    m0_col = m0_ref[:, 0:1]

    t_idx = _iota((BLOCK_ROWS, D_HEAD), 0) & (seg_len - 1)

    def conv_head(pre_ref, hist_ref, w_ref, b_ref):
        hist = hist_ref[...]

        def prev_fn(d, xd):
            back = (BLOCK_ROWS - (CONV_W - 1 - d)) % BLOCK_ROWS
            hd = hist if back == 0 else pltpu.roll(hist, back, axis=0)
            return jnp.where(t_idx >= d, xd, hd)

        return _conv_taps(pre_ref[...].astype(F32), w_ref[...], b_ref[...], prev_fn)

    qh = conv_head(qpre_ref, histq_ref, wq_ref, bq_ref)
    kh = conv_head(kpre_ref, histk_ref, wk_ref, bk_ref) * (D_HEAD ** -0.5)
    qb = qh.astype(BF16)
    kb = kh.astype(BF16)
    vh_b = v_ref[...]

    src_col = blast_col - bt + ig_col
    src_row = blast_row - bs + ig_row
    seg_max = jnp.max(jnp.where(same, src_row, -jnp.inf), axis=1, keepdims=True)
    m_new_col = jnp.maximum(blast_col + m0_col, seg_max)
    ws_col = jnp.exp(src_col - m_new_col)
    decay_col = jnp.exp(blast_col + m0_col - m_new_col)
    dec_scr[...] = jnp.broadcast_to(decay_col, (BLOCK_ROWS, LANES))
    vwt = (vh_b.astype(F32) * ws_col).T

    row_seg = _iota((BLOCK_ROWS, D_HEAD), 0) >> seg_shift
    lane_seg = _iota((D_HEAD, BLOCK_ROWS), 1) >> seg_shift

    def per_sequence(b, acc):
        c_old = c0_ref[b, 0]
        q_b = jnp.where(row_seg == b, qh, 0.0).astype(BF16)
        acc = acc + lax.dot_general(q_b, c_old.astype(BF16), NT_DIMS, preferred_element_type=F32)
        vw_b = jnp.where(lane_seg == b, vwt, 0.0).astype(BF16)
        dec = dec_scr[pl.ds(b * seg_len, 1), 0:1]
        c_ref[b, 0] = dec * c_old + _dot(vw_b, kb)
        return acc

    num_inter = lax.fori_loop(0, nseg, per_sequence, jnp.zeros((BLOCK_ROWS, D_HEAD), F32))

    n_old = n0_ref[...]
    den_inter = jnp.sum(qh * n_old, axis=1, keepdims=True)
    s_qk = lax.dot_general(qb, kb, NT_DIMS, preferred_element_type=F32)
    hcell = _cell_output(s_qk, vh_b, bt, bs, ig_row, m0_col, causal, num_inter, den_inter)
    ya_ref[...] = _gated_head_out(hcell, ghead_ref[...], o_ref[...], za_ref[...])

    n_ref[...] = decay_col * n_old + _mask_dot_left(same_b, ws_col * kh)
    m_ref[...] = jnp.broadcast_to(m_new_col, (BLOCK_ROWS, LANES))


def _mlstm_decode(proj, gates, gbias, hist, c0, n0_rows, m0_rows, w_conv, b_conv, g_head_row,
                  seg_len, row0):
    rows = hist.shape[0]
    nblk = rows // BLOCK_ROWS
    nseg = BLOCK_ROWS // seg_len
    blk0 = row0 // BLOCK_ROWS

    def stream(kind):
        return pl.BlockSpec((BLOCK_ROWS, D_HEAD), lambda i, h: (blk0 + i, kind * N_HEADS + h))

    def head_cols(nrows, off):
        return pl.BlockSpec((nrows, D_HEAD), lambda i, h: (0, off * N_HEADS + h))

    row_head = pl.BlockSpec((BLOCK_ROWS, D_HEAD), lambda i, h: (i, h))
    state = pl.BlockSpec((nseg, 1, D_HEAD, D_HEAD), lambda i, h: (i, h, 0, 0))
    m_rows = pl.BlockSpec((None, BLOCK_ROWS, LANES), lambda i, h: (h, i, 0))

    return pl.pallas_call(
        functools.partial(_mlstm_decode_kernel, seg_len=seg_len),
        grid=(nblk, N_HEADS),
        in_specs=[
            stream(COL_Q), stream(COL_K), stream(COL_V), stream(COL_O), stream(COL_ZA),
            pl.BlockSpec((BLOCK_ROWS, LANES), lambda i, h: (blk0 + i, 0)),
            pl.BlockSpec((1, LANES), lambda i, h: (0, 0)),
            pl.BlockSpec((BLOCK_ROWS, D_HEAD), lambda i, h: (i, h)),
            pl.BlockSpec((BLOCK_ROWS, D_HEAD), lambda i, h: (i, N_HEADS + h)),
            state, row_head, m_rows,
            head_cols(CONV_W, 0), head_cols(CONV_W, 1), head_cols(1, 0), head_cols(1, 1),
            head_cols(1, 0),
        ],
        out_specs=[row_head, state, row_head, m_rows],
        out_shape=[
            _sds((rows, D_MODEL), BF16),
            _sds(c0.shape, F32),
            _sds((rows, D_MODEL), F32),
            _sds((N_HEADS, rows, LANES), F32),
        ],
        scratch_shapes=[pltpu.VMEM((3, BLOCK_ROWS, BLOCK_ROWS), F32),
                        pltpu.VMEM((BLOCK_ROWS, LANES), F32)],
        compiler_params=pltpu.CompilerParams(
            dimension_semantics=("arbitrary", "arbitrary"),
            vmem_limit_bytes=VMEM_LIMIT_BYTES),
        name="mlstm_decode",
    )(proj, proj, proj, proj, proj, gates, gbias, hist, hist, c0, n0_rows, m0_rows,
      w_conv, w_conv, b_conv, b_conv, g_head_row)


def _merge_kernel(ya_ref, u_ref, vb_ref, zb_ref, ga_ref, gb_ref, x_ref, lng_ref, lnb_ref,
                  ws_ref, bcol_ref, wpa_ref, wpb_ref, wout_ref, gpost_ref,
                  y_ref, vrows_ref, yb_scr, *, seg_len, vrows_every, tm):
    _, causal, _ = _segment_masks(seg_len)
    vb = vb_ref[...].astype(F32)
    mu = jnp.mean(vb, axis=-1, keepdims=True)
    xc = vb - mu
    vbn = xc * lax.rsqrt(jnp.mean(xc * xc, axis=-1, keepdims=True) + EPS) * lng_ref[...] + lnb_ref[...]
    vbn_b = vbn.astype(BF16)

    gcols = D_MODEL // N_GROUPS
    for grp in range(N_GROUPS):
        w_g = jnp.where(causal, ws_ref[grp], 0.0).astype(BF16)
        bias = bcol_ref[:, grp:grp + 1]
        cols = slice(grp * gcols, (grp + 1) * gcols)
        for blk in range(tm // BLOCK_ROWS):
            rows = slice(blk * BLOCK_ROWS, (blk + 1) * BLOCK_ROWS)
            s = _dot(w_g, vbn_b[rows, cols]) + bias
            yb = u_ref[rows, cols].astype(F32) * s * _silu(zb_ref[rows, cols].astype(F32))
            yb_scr[rows, cols] = yb.astype(BF16)

    pa = _dot(ya_ref[...], wpa_ref[...])
    pb = _dot(yb_scr[...], wpb_ref[...])
    merged = (jax.nn.sigmoid(ga_ref[...].astype(F32)) * pa
              + jax.nn.sigmoid(gb_ref[...].astype(F32)) * pb)
    out = _dot(merged.astype(BF16), wout_ref[...])
    post = out * lax.rsqrt(jnp.mean(out * out, axis=-1, keepdims=True) + EPS) * gpost_ref[...]
    y_ref[...] = x_ref[...] + post

    if vrows_every == 1:
        vrows_ref[...] = vbn
    else:
        @pl.when(pl.program_id(0) % vrows_every == vrows_every - 1)
        def _():
            vrows_ref[...] = vbn[tm - BLOCK_ROWS:, :]


def _merge(ya2d, proj, x2d, ln_g, ln_b, w_eff, bcol, w_pa, w_pb, w_out, g_post,
           seg_len, rows_per_seq, tm, row0):
    rows = x2d.shape[0]
    blk0 = row0 // tm
    if rows_per_seq:
        vrows_every = rows_per_seq // tm
        vrows_shape = (rows // rows_per_seq * BLOCK_ROWS, D_MODEL)
        vrows_spec = pl.BlockSpec((BLOCK_ROWS, D_MODEL), lambda i: (i // vrows_every, 0))
    else:
        vrows_every = 1
        vrows_shape = (rows, D_MODEL)
        vrows_spec = pl.BlockSpec((tm, D_MODEL), lambda i: (i, 0))

    def stream(kind):
        return pl.BlockSpec((tm, D_MODEL), lambda i: (blk0 + i, kind))

    def const(shape, single=False):
        nd = len(shape)
        if single:
            return pl.BlockSpec(shape, lambda i: (0,) * nd, pipeline_mode=pl.Buffered(1))
        return pl.BlockSpec(shape, lambda i: (0,) * nd)

    weight = const((D_MODEL, D_MODEL), single=True)
    return pl.pallas_call(
        functools.partial(_merge_kernel, seg_len=seg_len, vrows_every=vrows_every, tm=tm),
        grid=(rows // tm,),
        in_specs=[
            pl.BlockSpec((tm, D_MODEL), lambda i: (i, 0)),
            stream(COL_U), stream(COL_VB), stream(COL_ZB), stream(COL_GA), stream(COL_GB),
            pl.BlockSpec((tm, D_MODEL), lambda i: (i, 0)),
            const((1, D_MODEL)), const((1, D_MODEL)),
            const((N_GROUPS, BLOCK_ROWS, BLOCK_ROWS)), const((BLOCK_ROWS, LANES)),
            weight, weight, weight,
            const((1, D_MODEL)),
        ],
        out_specs=[pl.BlockSpec((tm, D_MODEL), lambda i: (i, 0)), vrows_spec],
        out_shape=[_sds((rows, D_MODEL), F32), _sds(vrows_shape, F32)],
        scratch_shapes=[pltpu.VMEM((tm, D_MODEL), BF16)],
        compiler_params=pltpu.CompilerParams(
            dimension_semantics=("arbitrary",),
            vmem_limit_bytes=VMEM_LIMIT_BYTES),
        name="merge",
    )(ya2d, proj, proj, proj, proj, proj, x2d, ln_g, ln_b, w_eff, bcol,
      w_pa, w_pb, w_out, g_post)


PROJ_ROWS_MAX = 1280
BF16_SUBLANES = 16


def _proj_rows(rows):
    best = BF16_SUBLANES
    for tm in range(BF16_SUBLANES, PROJ_ROWS_MAX + 1, BF16_SUBLANES):
        if rows % tm == 0:
            best = tm
    return best


def _pad_lanes(a):
    return jnp.pad(a, ((0, 0), (0, LANES - a.shape[1])))


def kernel(x_prompt, x_sample, state_mlstm_C, state_mlstm_n, state_mlstm_m, state_conv_qk,
           g_pre, w_in, b_ig, b_fg, w_conv, b_conv, g_head, ln_g, ln_b, w_s, b_s,
           w_pa, w_pb, w_out, g_post):
    batch, seq, _ = x_prompt.shape
    dec_batch, dec_seq, _ = x_sample.shape
    dec_rows = dec_batch * dec_seq
    assert seq % BLOCK_ROWS == 0 and BLOCK_ROWS % dec_seq == 0 and dec_rows % BLOCK_ROWS == 0
    assert dec_seq >= CONV_W - 1 and dec_seq & (dec_seq - 1) == 0

    gbias = _pad_lanes(jnp.concatenate([b_ig, b_fg])[None, :])
    g_pre2 = g_pre[None, :]
    b_conv2 = b_conv[None, :]
    g_head_row = g_head.reshape(1, D_MODEL)
    ln_g2, ln_b2, g_post2 = ln_g[None, :], ln_b[None, :], g_post[None, :]
    w_pa_b, w_pb_b, w_out_b = w_pa.astype(BF16), w_pb.astype(BF16), w_out.astype(BF16)

    p_rows = batch * seq
    xp2d = x_prompt.reshape(p_rows, D_MODEL)
    xs2d = x_sample.reshape(dec_rows, D_MODEL)
    assert p_rows % NORM_ROWS == 0 and dec_rows % NORM_ROWS == 0
    h_all, gates = _normalize(xp2d, xs2d, g_pre2, w_in)
    all_rows = p_rows + dec_rows
    proj = _project(h_all, w_in, tm=_proj_rows(all_rows), tn=1024)

    zeros = functools.partial(jnp.zeros, dtype=F32)
    ya_p, c_p, n_p, m_pb = _mlstm_prompt(
        proj, gates, gbias,
        zeros((batch, CONV_W - 1, 2 * D_MODEL)), zeros((batch, N_HEADS, D_HEAD, D_HEAD)),
        zeros((batch, N_HEADS, D_HEAD)), zeros((batch, N_HEADS, LANES)),
        w_conv, b_conv2, g_head_row, batch, seq)
    y_p, vrows_p = _merge(ya_p.reshape(p_rows, D_MODEL), proj, xp2d, ln_g2, ln_b2,
                          w_s, _pad_lanes(b_s.T), w_pa_b, w_pb_b, w_out_b, g_post2,
                          seg_len=BLOCK_ROWS, rows_per_seq=seq, tm=256, row0=0)
    conv_p = proj[:p_rows, :2 * D_MODEL].reshape(batch, seq, 2 * D_MODEL)
    conv_p = conv_p[:, seq - (CONV_W - 1):].astype(F32)

    proj_s = proj[p_rows:, :2 * D_MODEL]
    hist = jnp.pad(state_conv_qk, ((0, 0), (0, dec_seq - (CONV_W - 1)), (0, 0)))
    hist = hist.reshape(dec_rows, 2 * D_MODEL)
    n0_rows = jnp.repeat(state_mlstm_n.reshape(dec_batch, D_MODEL), dec_seq, axis=0)
    m0_rows = jnp.broadcast_to(jnp.repeat(state_mlstm_m.T, dec_seq, axis=1)[:, :, None],
                               (N_HEADS, dec_rows, LANES))
    ya_s, c_s, n_rows, m_rows = _mlstm_decode(
        proj, gates, gbias, hist, state_mlstm_C, n0_rows, m0_rows, w_conv, b_conv2, g_head_row,
        seg_len=dec_seq, row0=p_rows)
    nrep = BLOCK_ROWS // dec_seq
    w_eff_s = jnp.tile(w_s[:, :dec_seq, :dec_seq], (1, nrep, nrep))
    bcol_s = _pad_lanes(jnp.tile(b_s[:, :dec_seq], (1, nrep)).T)
    y_s, vrows_s = _merge(ya_s, proj, xs2d, ln_g2, ln_b2, w_eff_s, bcol_s,
                          w_pa_b, w_pb_b, w_out_b, g_post2,
                          seg_len=dec_seq, rows_per_seq=0, tm=256, row0=p_rows)
    n_s = n_rows[::dec_seq].reshape(dec_batch, N_HEADS, D_HEAD)
    m_s = m_rows[:, ::dec_seq, 0].T
    conv_s = proj_s.reshape(dec_batch, dec_seq, 2 * D_MODEL)
    conv_s = conv_s[:, dec_seq - (CONV_W - 1):].astype(F32)

    return (y_p.reshape(batch, seq, D_MODEL), y_s.reshape(dec_batch, dec_seq, D_MODEL),
            c_p, n_p, m_pb[:, :, 0], conv_p, vrows_p.reshape(batch, BLOCK_ROWS, D_MODEL),
            c_s, n_s, m_s, conv_s, vrows_s.reshape(dec_batch, dec_seq, D_MODEL))
```

```python
import functools

import jax
import jax.numpy as jnp
from jax import lax
from jax.experimental import pallas as pl
from jax.experimental.pallas import tpu as pltpu

F32 = jnp.float32
BF16 = jnp.bfloat16

D_MODEL = 2048
N_HEADS = 8
D_HEAD = 256
N_GROUPS = 8
CONV_W = 4
EPS = 1e-6
BLOCK_ROWS = 128
LANES = 128
N_MAIN = 10 * D_MODEL
GATE_LO = 5 * D_MODEL
N_GATE_COLS = 2 * N_HEADS
NORM_ROWS = 512
W_CAST_ROWS = 256
COL_Q, COL_K, COL_V, COL_O, COL_ZA, COL_U, COL_VB, COL_ZB, COL_GA, COL_GB = range(10)

NT_DIMS = (((1,), (1,)), ((), ()))
TN_DIMS = (((0,), (0,)), ((), ()))

VMEM_LIMIT_BYTES = 56 * 1024 * 1024


def _sds(shape, dtype):
    return jax.ShapeDtypeStruct(shape, dtype)


def _iota(shape, dim):
    return lax.broadcasted_iota(jnp.int32, shape, dim)


def _dot(a, b):
    return jnp.dot(a, b, preferred_element_type=F32)


def _split_bf16(x):
    hi = x.astype(BF16)
    lo = (x - hi.astype(F32)).astype(BF16)
    return hi, lo


def _mask_dot_left(mask_b, x):
    hi, lo = _split_bf16(x)
    return _dot(mask_b, hi) + _dot(mask_b, lo)


def _mask_dot_right(x, mask_b):
    hi, lo = _split_bf16(x)
    return _dot(hi, mask_b) + _dot(lo, mask_b)


def _log_sigmoid(x):
    return -(jnp.maximum(-x, 0.0) + jnp.log1p(jnp.exp(-jnp.abs(x))))


def _silu(x):
    return x * jax.nn.sigmoid(x)


def _segment_masks(seg_len):
    r = _iota((BLOCK_ROWS, BLOCK_ROWS), 0)
    c = _iota((BLOCK_ROWS, BLOCK_ROWS), 1)
    shift = seg_len.bit_length() - 1
    same = (r >> shift) == (c >> shift)
    causal = jnp.logical_and(same, c <= r)
    causal_t = jnp.logical_and(same, r <= c)
    return same, causal, causal_t


def _as_bf16_mask(m):
    return jnp.where(m, 1.0, 0.0).astype(BF16)


def _norm_kernel(xp_ref, xs_ref, g_ref, wg_ref, h_ref, gate_ref, *, n_prompt_blocks):
    def emit(x):
        ms = jnp.mean(x * x, axis=-1, keepdims=True)
        h = (x * lax.rsqrt(ms + EPS) * g_ref[...]).astype(BF16)
        h_ref[...] = h
        gate_ref[...] = lax.dot_general(h, wg_ref[...].astype(BF16), NT_DIMS,
                                        preferred_element_type=F32)

    i = pl.program_id(0)

    @pl.when(i < n_prompt_blocks)
    def _():
        emit(xp_ref[...])

    @pl.when(i >= n_prompt_blocks)
    def _():
        emit(xs_ref[...])


def _normalize(xp2d, xs2d, g_pre, w_in_t):
    tm = NORM_ROWS
    n_prompt_blocks = xp2d.shape[0] // tm
    nblk = n_prompt_blocks + xs2d.shape[0] // tm
    rows = nblk * tm
    return pl.pallas_call(
        functools.partial(_norm_kernel, n_prompt_blocks=n_prompt_blocks),
        grid=(nblk,),
        in_specs=[
            pl.BlockSpec((tm, D_MODEL), lambda i: (jnp.minimum(i, n_prompt_blocks - 1), 0)),
            pl.BlockSpec((tm, D_MODEL), lambda i: (jnp.maximum(i - n_prompt_blocks, 0), 0)),
            pl.BlockSpec((1, D_MODEL), lambda i: (0, 0)),
            pl.BlockSpec((LANES, D_MODEL), lambda i: (GATE_LO // LANES, 0)),
        ],
        out_specs=[
            pl.BlockSpec((tm, D_MODEL), lambda i: (i, 0)),
            pl.BlockSpec((tm, LANES), lambda i: (i, 0)),
        ],
        out_shape=[_sds((rows, D_MODEL), BF16), _sds((rows, LANES), F32)],
        compiler_params=pltpu.CompilerParams(
            dimension_semantics=("arbitrary",), vmem_limit_bytes=VMEM_LIMIT_BYTES),
        name="norm",
    )(xp2d, xs2d, g_pre, w_in_t)


def _proj_kernel(h_ref, wa_ref, wb_ref, o_ref, w_scr, *, tn, first_shifted):
    j = pl.program_id(0)

    def cast_rows(dst_lo, src_ref, src_lo, n):
        w_scr[dst_lo:dst_lo + n, :] = src_ref[src_lo:src_lo + n, :].astype(BF16)

    @pl.when(pl.program_id(1) == 0)
    def _cast():
        @pl.when(j < first_shifted)
        def _():
            for lo in range(0, tn, W_CAST_ROWS):
                cast_rows(lo, wa_ref, lo, W_CAST_ROWS)

        @pl.when(j >= first_shifted)
        def _():
            for lo in range(0, tn - W_CAST_ROWS, W_CAST_ROWS):
                cast_rows(lo, wa_ref, lo + N_GATE_COLS, W_CAST_ROWS)
            cast_rows(tn - W_CAST_ROWS, wa_ref, tn - W_CAST_ROWS + N_GATE_COLS,
                      W_CAST_ROWS - N_GATE_COLS)
            cast_rows(tn - N_GATE_COLS, wb_ref, 0, N_GATE_COLS)

    o_ref[...] = lax.dot_general(h_ref[...], w_scr[...], NT_DIMS,
                                 preferred_element_type=F32).astype(BF16)


def _project(h_all, w_in_t, tm, tn):
    rows = h_all.shape[0]
    return pl.pallas_call(
        functools.partial(_proj_kernel, tn=tn, first_shifted=GATE_LO // tn),
        grid=(N_MAIN // tn, rows // tm),
        in_specs=[
            pl.BlockSpec((tm, D_MODEL), lambda j, i: (i, 0)),
            pl.BlockSpec((tn, D_MODEL), lambda j, i: (j, 0)),
            pl.BlockSpec((N_GATE_COLS, D_MODEL), lambda j, i: ((j + 1) * (tn // N_GATE_COLS), 0)),
        ],
        out_specs=pl.BlockSpec((tm, tn), lambda j, i: (i, j)),
        out_shape=_sds((rows, N_MAIN), BF16),
        scratch_shapes=[pltpu.VMEM((tn, D_MODEL), BF16)],
        compiler_params=pltpu.CompilerParams(
            dimension_semantics=("arbitrary", "arbitrary"),
            vmem_limit_bytes=VMEM_LIMIT_BYTES),
        name="proj",
    )(h_all, w_in_t, w_in_t)


def _gate_prep(graw, gbias, causal_b, causal_t_b):
    g = graw + gbias
    bcum = _mask_dot_left(causal_b, _log_sigmoid(g))
    gt = g.T
    bcum_t = _mask_dot_right(_log_sigmoid(gt), causal_t_b)
    return g, bcum, gt, bcum_t


def _conv_taps(x, w, bias, prev_fn):
    acc = bias + w[CONV_W - 1:CONV_W] * x
    for d in range(1, CONV_W):
        xd = prev_fn(d, pltpu.roll(x, d, axis=0))
        acc = acc + w[CONV_W - 1 - d:CONV_W - d] * xd
    return _silu(acc)


def _cell_output(s_qk, vh_b, bt, bs, ig_row, mprev_col, causal, num_inter, den_inter):
    dmat = jnp.where(causal, bt - bs + ig_row, -jnp.inf)
    inter = bt + mprev_col
    m_row = jnp.maximum(jnp.max(dmat, axis=1, keepdims=True), inter)
    wts = jnp.exp(dmat - m_row)
    sc = jnp.exp(inter - m_row)
    s = s_qk * wts
    num = _dot(s.astype(BF16), vh_b) + sc * num_inter
    den = jnp.sum(s, axis=1, keepdims=True) + sc * den_inter
    inv = 1.0 / jnp.maximum(jnp.abs(den), jnp.exp(-m_row))
    return num * inv


def _gated_head_out(hcell, ghead_row, o_b, za_b):
    hn = hcell * lax.rsqrt(jnp.mean(hcell * hcell, axis=-1, keepdims=True) + EPS) * ghead_row
    return (jax.nn.sigmoid(o_b.astype(F32)) * hn * _silu(za_b.astype(F32))).astype(BF16)


def _mlstm_prompt_kernel(qpre_ref, kpre_ref, v_ref, o_ref, za_ref, gate_ref, gbias_ref, buf_ref,
                         c0_ref, n0_ref, m0_ref, wconv_ref, bconv_ref, ghead_ref,
                         ya_ref, c_ref, n_ref, m_ref, tail_scr):
    chunk = pl.program_id(1)

    @pl.when(chunk == 0)
    def _init():
        c_ref[...] = c0_ref[...]
        n_ref[...] = n0_ref[...]
        m_ref[...] = m0_ref[...]
        tail_scr[...] = jnp.zeros_like(tail_scr)
        for d in range(1, CONV_W):
            tail_scr[d - 1, 0:d, :] = buf_ref[CONV_W - 1 - d:CONV_W - 1, :]

    _, causal, causal_t = _segment_masks(BLOCK_ROWS)
    g, bcum, gt, bcum_t = _gate_prep(gate_ref[...], gbias_ref[...],
                                     _as_bf16_mask(causal), _as_bf16_mask(causal_t))
    row8 = _iota((8, D_HEAD), 0)

    for h in range(N_HEADS):
        cols = slice(h * D_HEAD, (h + 1) * D_HEAD)

        def conv_head(pre_ref, off):
            ccols = slice(off + h * D_HEAD, off + (h + 1) * D_HEAD)

            def prev_fn(d, xd):
                top = jnp.where(row8 < d, tail_scr[d - 1, :, ccols], xd[0:8])
                tail_scr[d - 1, :, ccols] = xd[0:8]
                return jnp.concatenate([top, xd[8:]], axis=0)

            return _conv_taps(pre_ref[:, cols].astype(F32), wconv_ref[:, ccols],
                              bconv_ref[:, ccols], prev_fn)

        qh = conv_head(qpre_ref, 0)
        kh = conv_head(kpre_ref, D_MODEL) * (D_HEAD ** -0.5)
        qb = qh.astype(BF16)
        kb = kh.astype(BF16)
        vh_b = v_ref[:, cols]

        bt = bcum[:, N_HEADS + h:N_HEADS + h + 1]
        bs = bcum_t[N_HEADS + h:N_HEADS + h + 1, :]
        ig_col = g[:, h:h + 1]
        ig_row = gt[h:h + 1, :]
        mprev = m_ref[h:h + 1, 0:1]
        c_old = c_ref[h]
        n_old = n_ref[h:h + 1, :]

        s_qk = lax.dot_general(qb, kb, NT_DIMS, preferred_element_type=F32)
        num_inter = lax.dot_general(qb, c_old.astype(BF16), NT_DIMS, preferred_element_type=F32)
        den_inter = jnp.sum(qh * n_old, axis=1, keepdims=True)
        hcell = _cell_output(s_qk, vh_b, bt, bs, ig_row, mprev, causal, num_inter, den_inter)
        ya_ref[:, cols] = _gated_head_out(hcell, ghead_ref[:, cols], o_ref[:, cols], za_ref[:, cols])

        b_last = bt[BLOCK_ROWS - 1:BLOCK_ROWS, :]
        src_row = b_last - bs + ig_row
        m_new = jnp.maximum(b_last + mprev, jnp.max(src_row, axis=1, keepdims=True))
        ws_col = jnp.exp(b_last - bt + ig_col - m_new)
        decay = jnp.exp(b_last + mprev - m_new)
        vw = (vh_b.astype(F32) * ws_col).astype(BF16)
        c_ref[h] = decay * c_old + lax.dot_general(vw, kb, TN_DIMS, preferred_element_type=F32)
        n_ref[h:h + 1, :] = decay * n_old + jnp.sum(ws_col * kh, axis=0, keepdims=True)
        m_ref[h:h + 1, :] = jnp.broadcast_to(m_new, (1, LANES))


def _mlstm_prompt(proj, gates, gbias, buf0, c0, n0, m0b, w_conv, b_conv, g_head_row, batch, seq):
    nchunk = seq // BLOCK_ROWS

    def stream(kind):
        return pl.BlockSpec((BLOCK_ROWS, D_MODEL), lambda b, c: (b * nchunk + c, kind))

    def per_batch(shape):
        nd = len(shape)
        return pl.BlockSpec((None,) + shape, lambda b, c: (b,) + (0,) * nd)

    def const(shape):
        nd = len(shape)
        return pl.BlockSpec(shape, lambda b, c: (0,) * nd)

    return pl.pallas_call(
        _mlstm_prompt_kernel,
        grid=(batch, nchunk),
        in_specs=[
            stream(COL_Q), stream(COL_K), stream(COL_V), stream(COL_O), stream(COL_ZA),
            pl.BlockSpec((BLOCK_ROWS, LANES), lambda b, c: (b * nchunk + c, 0)),
            const((1, LANES)),
            per_batch((CONV_W - 1, 2 * D_MODEL)),
            per_batch((N_HEADS, D_HEAD, D_HEAD)),
            per_batch((N_HEADS, D_HEAD)),
            per_batch((N_HEADS, LANES)),
            const((CONV_W, 2 * D_MODEL)),
            const((1, 2 * D_MODEL)),
            const((1, D_MODEL)),
        ],
        out_specs=[
            pl.BlockSpec((None, BLOCK_ROWS, D_MODEL), lambda b, c: (b, c, 0)),
            per_batch((N_HEADS, D_HEAD, D_HEAD)),
            per_batch((N_HEADS, D_HEAD)),
            per_batch((N_HEADS, LANES)),
        ],
        out_shape=[
            _sds((batch, seq, D_MODEL), BF16),
            _sds((batch, N_HEADS, D_HEAD, D_HEAD), F32),
            _sds((batch, N_HEADS, D_HEAD), F32),
            _sds((batch, N_HEADS, LANES), F32),
        ],
        scratch_shapes=[pltpu.VMEM((CONV_W - 1, 8, 2 * D_MODEL), F32)],
        compiler_params=pltpu.CompilerParams(
            dimension_semantics=("arbitrary", "arbitrary"),
            vmem_limit_bytes=VMEM_LIMIT_BYTES),
        name="mlstm_prompt",
    )(proj, proj, proj, proj, proj, gates, gbias, buf0, c0, n0, m0b,
      w_conv, b_conv, g_head_row)


def _mlstm_decode_kernel(qpre_ref, kpre_ref, v_ref, o_ref, za_ref, gate_ref, gbias_ref,
                         histq_ref, histk_ref, c0_ref, n0_ref, m0_ref,
                         wq_ref, wk_ref, bq_ref, bk_ref, ghead_ref,
                         ya_ref, c_ref, n_ref, m_ref, rows_scr, dec_scr, *, seg_len):
    h = pl.program_id(1)
    nseg = BLOCK_ROWS // seg_len
    seg_shift = seg_len.bit_length() - 1
    same, causal, causal_t = _segment_masks(seg_len)
    same_b = _as_bf16_mask(same)
    g, bcum, gt, bcum_t = _gate_prep(gate_ref[...], gbias_ref[...],
                                     _as_bf16_mask(causal), _as_bf16_mask(causal_t))
    lf = _log_sigmoid(g)
    btot = _mask_dot_left(same_b, lf)
    btot_t = _mask_dot_right(_log_sigmoid(gt), same_b)
    rows_scr[0] = gt
    rows_scr[1] = bcum_t
    rows_scr[2] = btot_t
    ig_row = rows_scr[0, pl.ds(h, 1), :]
    bs = rows_scr[1, pl.ds(N_HEADS + h, 1), :]
    blast_row = rows_scr[2, pl.ds(N_HEADS + h, 1), :]

    lane = _iota((BLOCK_ROWS, LANES), 1)

    def pick_lane(x, idx):
        return jnp.sum(jnp.where(lane == idx, x, 0.0), axis=1, keepdims=True)

    ig_col = pick_lane(g, h)
    bt = pick_lane(bcum, N_HEADS + h)
    blast_col = pick_lane(btot, N_HEADS + h)
    m0_col = m0_ref[:, 0:1]

    t_idx = _iota((BLOCK_ROWS, D_HEAD), 0) & (seg_len - 1)

    def conv_head(pre_ref, hist_ref, w_ref, b_ref):
        hist = hist_ref[...]

        def prev_fn(d, xd):
            back = (BLOCK_ROWS - (CONV_W - 1 - d)) % BLOCK_ROWS
            hd = hist if back == 0 else pltpu.roll(hist, back, axis=0)
            return jnp.where(t_idx >= d, xd, hd)

        return _conv_taps(pre_ref[...].astype(F32), w_ref[...], b_ref[...], prev_fn)

    qh = conv_head(qpre_ref, histq_ref, wq_ref, bq_ref)
    kh = conv_head(kpre_ref, histk_ref, wk_ref, bk_ref) * (D_HEAD ** -0.5)
    qb = qh.astype(BF16)
    kb = kh.astype(BF16)
    vh_b = v_ref[...]

    src_col = blast_col - bt + ig_col
    src_row = blast_row - bs + ig_row
    seg_max = jnp.max(jnp.where(same, src_row, -jnp.inf), axis=1, keepdims=True)
    m_new_col = jnp.maximum(blast_col + m0_col, seg_max)
    ws_col = jnp.exp(src_col - m_new_col)
    decay_col = jnp.exp(blast_col + m0_col - m_new_col)
    dec_scr[...] = jnp.broadcast_to(decay_col, (BLOCK_ROWS, LANES))
    vwt = (vh_b.astype(F32) * ws_col).T

    row_seg = _iota((BLOCK_ROWS, D_HEAD), 0) >> seg_shift
    lane_seg = _iota((D_HEAD, BLOCK_ROWS), 1) >> seg_shift

    def per_sequence(b, acc):
        c_old = c0_ref[b, 0]
        q_b = jnp.where(row_seg == b, qh, 0.0).astype(BF16)
        acc = acc + lax.dot_general(q_b, c_old.astype(BF16), NT_DIMS, preferred_element_type=F32)
        vw_b = jnp.where(lane_seg == b, vwt, 0.0).astype(BF16)
        dec = dec_scr[pl.ds(b * seg_len, 1), 0:1]
        c_ref[b, 0] = dec * c_old + _dot(vw_b, kb)
        return acc

    num_inter = lax.fori_loop(0, nseg, per_sequence, jnp.zeros((BLOCK_ROWS, D_HEAD), F32))

    n_old = n0_ref[...]
    den_inter = jnp.sum(qh * n_old, axis=1, keepdims=True)
    s_qk = lax.dot_general(qb, kb, NT_DIMS, preferred_element_type=F32)
    hcell = _cell_output(s_qk, vh_b, bt, bs, ig_row, m0_col, causal, num_inter, den_inter)
    ya_ref[...] = _gated_head_out(hcell, ghead_ref[...], o_ref[...], za_ref[...])

    n_ref[...] = decay_col * n_old + _mask_dot_left(same_b, ws_col * kh)
    m_ref[...] = jnp.broadcast_to(m_new_col, (BLOCK_ROWS, LANES))


def _mlstm_decode(proj, gates, gbias, hist, c0, n0_rows, m0_rows, w_conv, b_conv, g_head_row,
                  seg_len, row0):
    rows = hist.shape[0]
    nblk = rows // BLOCK_ROWS
    nseg = BLOCK_ROWS // seg_len
    blk0 = row0 // BLOCK_ROWS

    def stream(kind):
        return pl.BlockSpec((BLOCK_ROWS, D_HEAD), lambda i, h: (blk0 + i, kind * N_HEADS + h))

    def head_cols(nrows, off):
        return pl.BlockSpec((nrows, D_HEAD), lambda i, h: (0, off * N_HEADS + h))

    row_head = pl.BlockSpec((BLOCK_ROWS, D_HEAD), lambda i, h: (i, h))
    state = pl.BlockSpec((nseg, 1, D_HEAD, D_HEAD), lambda i, h: (i, h, 0, 0))
    m_rows = pl.BlockSpec((None, BLOCK_ROWS, LANES), lambda i, h: (h, i, 0))

    return pl.pallas_call(
        functools.partial(_mlstm_decode_kernel, seg_len=seg_len),
        grid=(nblk, N_HEADS),
        in_specs=[
            stream(COL_Q), stream(COL_K), stream(COL_V), stream(COL_O), stream(COL_ZA),
            pl.BlockSpec((BLOCK_ROWS, LANES), lambda i, h: (blk0 + i, 0)),
            pl.BlockSpec((1, LANES), lambda i, h: (0, 0)),
            pl.BlockSpec((BLOCK_ROWS, D_HEAD), lambda i, h: (i, h)),
            pl.BlockSpec((BLOCK_ROWS, D_HEAD), lambda i, h: (i, N_HEADS + h)),
            state, row_head, m_rows,
            head_cols(CONV_W, 0), head_cols(CONV_W, 1), head_cols(1, 0), head_cols(1, 1),
            head_cols(1, 0),
        ],
        out_specs=[row_head, state, row_head, m_rows],
        out_shape=[
            _sds((rows, D_MODEL), BF16),
            _sds(c0.shape, F32),
            _sds((rows, D_MODEL), F32),
            _sds((N_HEADS, rows, LANES), F32),
        ],
        scratch_shapes=[pltpu.VMEM((3, BLOCK_ROWS, BLOCK_ROWS), F32),
                        pltpu.VMEM((BLOCK_ROWS, LANES), F32)],
        compiler_params=pltpu.CompilerParams(
            dimension_semantics=("arbitrary", "arbitrary"),
            vmem_limit_bytes=VMEM_LIMIT_BYTES),
        name="mlstm_decode",
    )(proj, proj, proj, proj, proj, gates, gbias, hist, hist, c0, n0_rows, m0_rows,
      w_conv, w_conv, b_conv, b_conv, g_head_row)


def _merge_kernel(ya_ref, u_ref, vb_ref, zb_ref, ga_ref, gb_ref, x_ref, lng_ref, lnb_ref,
                  ws_ref, bcol_ref, wpa_ref, wpb_ref, wout_ref, gpost_ref,
                  y_ref, vrows_ref, yb_scr, *, seg_len, vrows_every, tm):
    _, causal, _ = _segment_masks(seg_len)
    vb = vb_ref[...].astype(F32)
    mu = jnp.mean(vb, axis=-1, keepdims=True)
    xc = vb - mu
    vbn = xc * lax.rsqrt(jnp.mean(xc * xc, axis=-1, keepdims=True) + EPS) * lng_ref[...] + lnb_ref[...]
    vbn_b = vbn.astype(BF16)

    gcols = D_MODEL // N_GROUPS
    for grp in range(N_GROUPS):
        w_g = jnp.where(causal, ws_ref[grp], 0.0).astype(BF16)
        bias = bcol_ref[:, grp:grp + 1]
        cols = slice(grp * gcols, (grp + 1) * gcols)
        for blk in range(tm // BLOCK_ROWS):
            rows = slice(blk * BLOCK_ROWS, (blk + 1) * BLOCK_ROWS)
            s = _dot(w_g, vbn_b[rows, cols]) + bias
            yb = u_ref[rows, cols].astype(F32) * s * _silu(zb_ref[rows, cols].astype(F32))
            yb_scr[rows, cols] = yb.astype(BF16)

    pa = _dot(ya_ref[...], wpa_ref[...])
    pb = _dot(yb_scr[...], wpb_ref[...])
    merged = (jax.nn.sigmoid(ga_ref[...].astype(F32)) * pa
              + jax.nn.sigmoid(gb_ref[...].astype(F32)) * pb)
    out = _dot(merged.astype(BF16), wout_ref[...])
    post = out * lax.rsqrt(jnp.mean(out * out, axis=-1, keepdims=True) + EPS) * gpost_ref[...]
    y_ref[...] = x_ref[...] + post

    if vrows_every == 1:
        vrows_ref[...] = vbn
    else:
        @pl.when(pl.program_id(0) % vrows_every == vrows_every - 1)
        def _():
            vrows_ref[...] = vbn[tm - BLOCK_ROWS:, :]


def _merge(ya2d, proj, x2d, ln_g, ln_b, w_eff, bcol, w_pa, w_pb, w_out, g_post,
           seg_len, rows_per_seq, tm, row0):
    rows = x2d.shape[0]
    blk0 = row0 // tm
    if rows_per_seq:
        vrows_every = rows_per_seq // tm
        vrows_shape = (rows // rows_per_seq * BLOCK_ROWS, D_MODEL)
        vrows_spec = pl.BlockSpec((BLOCK_ROWS, D_MODEL), lambda i: (i // vrows_every, 0))
    else:
        vrows_every = 1
        vrows_shape = (rows, D_MODEL)
        vrows_spec = pl.BlockSpec((tm, D_MODEL), lambda i: (i, 0))

    def stream(kind):
        return pl.BlockSpec((tm, D_MODEL), lambda i: (blk0 + i, kind))

    def const(shape, single=False):
        nd = len(shape)
        if single:
            return pl.BlockSpec(shape, lambda i: (0,) * nd, pipeline_mode=pl.Buffered(1))
        return pl.BlockSpec(shape, lambda i: (0,) * nd)

    weight = const((D_MODEL, D_MODEL), single=True)
    return pl.pallas_call(
        functools.partial(_merge_kernel, seg_len=seg_len, vrows_every=vrows_every, tm=tm),
        grid=(rows // tm,),
        in_specs=[
            pl.BlockSpec((tm, D_MODEL), lambda i: (i, 0)),
            stream(COL_U), stream(COL_VB), stream(COL_ZB), stream(COL_GA), stream(COL_GB),
            pl.BlockSpec((tm, D_MODEL), lambda i: (i, 0)),
            const((1, D_MODEL)), const((1, D_MODEL)),
            const((N_GROUPS, BLOCK_ROWS, BLOCK_ROWS)), const((BLOCK_ROWS, LANES)),
            weight, weight, weight,
            const((1, D_MODEL)),
        ],
        out_specs=[pl.BlockSpec((tm, D_MODEL), lambda i: (i, 0)), vrows_spec],
        out_shape=[_sds((rows, D_MODEL), F32), _sds(vrows_shape, F32)],
        scratch_shapes=[pltpu.VMEM((tm, D_MODEL), BF16)],
        compiler_params=pltpu.CompilerParams(
            dimension_semantics=("arbitrary",),
            vmem_limit_bytes=VMEM_LIMIT_BYTES),
        name="merge",
    )(ya2d, proj, proj, proj, proj, proj, x2d, ln_g, ln_b, w_eff, bcol,
      w_pa, w_pb, w_out, g_post)


PROJ_ROWS_MAX = 1280
BF16_SUBLANES = 16


def _proj_rows(rows):
    best = BF16_SUBLANES
    for tm in range(BF16_SUBLANES, PROJ_ROWS_MAX + 1, BF16_SUBLANES):
        if rows % tm == 0:
            best = tm
    return best


def _pad_lanes(a):
    return jnp.pad(a, ((0, 0), (0, LANES - a.shape[1])))


def kernel(x_prompt, x_sample, state_mlstm_C, state_mlstm_n, state_mlstm_m, state_conv_qk,
           g_pre, w_in, b_ig, b_fg, w_conv, b_conv, g_head, ln_g, ln_b, w_s, b_s,
           w_pa, w_pb, w_out, g_post):
    batch, seq, _ = x_prompt.shape
    dec_batch, dec_seq, _ = x_sample.shape
    dec_rows = dec_batch * dec_seq
    assert seq % BLOCK_ROWS == 0 and BLOCK_ROWS % dec_seq == 0 and dec_rows % BLOCK_ROWS == 0
    assert dec_seq >= CONV_W - 1 and dec_seq & (dec_seq - 1) == 0

    gbias = _pad_lanes(jnp.concatenate([b_ig, b_fg])[None, :])
    g_pre2 = g_pre[None, :]
    b_conv2 = b_conv[None, :]
    g_head_row = g_head.reshape(1, D_MODEL)
    ln_g2, ln_b2, g_post2 = ln_g[None, :], ln_b[None, :], g_post[None, :]
    w_pa_b, w_pb_b, w_out_b = w_pa.astype(BF16), w_pb.astype(BF16), w_out.astype(BF16)

    p_rows = batch * seq
    xp2d = x_prompt.reshape(p_rows, D_MODEL)
    xs2d = x_sample.reshape(dec_rows, D_MODEL)
    assert p_rows % NORM_ROWS == 0 and dec_rows % NORM_ROWS == 0
    w_in_t = w_in.T
    h_all, gates = _normalize(xp2d, xs2d, g_pre2, w_in_t)
    all_rows = p_rows + dec_rows
    proj = _project(h_all, w_in_t, tm=_proj_rows(all_rows), tn=1024)

    zeros = functools.partial(jnp.zeros, dtype=F32)
    ya_p, c_p, n_p, m_pb = _mlstm_prompt(
        proj, gates, gbias,
        zeros((batch, CONV_W - 1, 2 * D_MODEL)), zeros((batch, N_HEADS, D_HEAD, D_HEAD)),
        zeros((batch, N_HEADS, D_HEAD)), zeros((batch, N_HEADS, LANES)),
        w_conv, b_conv2, g_head_row, batch, seq)
    y_p, vrows_p = _merge(ya_p.reshape(p_rows, D_MODEL), proj, xp2d, ln_g2, ln_b2,
                          w_s, _pad_lanes(b_s.T), w_pa_b, w_pb_b, w_out_b, g_post2,
                          seg_len=BLOCK_ROWS, rows_per_seq=seq, tm=256, row0=0)
    conv_p = jnp.stack([proj[(b + 1) * seq - (CONV_W - 1):(b + 1) * seq, :2 * D_MODEL]
                        for b in range(batch)]).astype(F32)

    proj_s = proj[p_rows:, :2 * D_MODEL]
    hist = jnp.pad(state_conv_qk, ((0, 0), (0, dec_seq - (CONV_W - 1)), (0, 0)))
    hist = hist.reshape(dec_rows, 2 * D_MODEL)
    n0_rows = jnp.repeat(state_mlstm_n.reshape(dec_batch, D_MODEL), dec_seq, axis=0)
    m0_rows = jnp.broadcast_to(jnp.repeat(state_mlstm_m.T, dec_seq, axis=1)[:, :, None],
                               (N_HEADS, dec_rows, LANES))
    ya_s, c_s, n_rows, m_rows = _mlstm_decode(
        proj, gates, gbias, hist, state_mlstm_C, n0_rows, m0_rows, w_conv, b_conv2, g_head_row,
        seg_len=dec_seq, row0=p_rows)
    nrep = BLOCK_ROWS // dec_seq
    w_eff_s = jnp.tile(w_s[:, :dec_seq, :dec_seq], (1, nrep, nrep))
    bcol_s = _pad_lanes(jnp.tile(b_s[:, :dec_seq], (1, nrep)).T)
    y_s, vrows_s = _merge(ya_s, proj, xs2d, ln_g2, ln_b2, w_eff_s, bcol_s,
                          w_pa_b, w_pb_b, w_out_b, g_post2,
                          seg_len=dec_seq, rows_per_seq=0, tm=256, row0=p_rows)
    n_s = n_rows[::dec_seq].reshape(dec_batch, N_HEADS, D_HEAD)
    m_s = m_rows[:, ::dec_seq, 0].T
    conv_s = proj_s.reshape(dec_batch, dec_seq, 2 * D_MODEL)
    conv_s = conv_s[:, dec_seq - (CONV_W - 1):].astype(F32)

    return (y_p.reshape(batch, seq, D_MODEL), y_s.reshape(dec_batch, dec_seq, D_MODEL),
            c_p, n_p, m_pb[:, :, 0], conv_p, vrows_p.reshape(batch, BLOCK_ROWS, D_MODEL),
            c_s, n_s, m_s, conv_s, vrows_s.reshape(dec_batch, dec_seq, D_MODEL))
```

```python
import functools

import jax
import jax.numpy as jnp
from jax import lax
from jax.experimental import pallas as pl
from jax.experimental.pallas import tpu as pltpu

F32 = jnp.float32
BF16 = jnp.bfloat16

D_MODEL = 2048
N_HEADS = 8
D_HEAD = 256
N_GROUPS = 8
CONV_W = 4
EPS = 1e-6
BLOCK_ROWS = 128
LANES = 128
N_MAIN = 10 * D_MODEL
GATE_LO = 5 * D_MODEL
N_GATE_COLS = 2 * N_HEADS
NORM_ROWS = 512
W_CAST_ROWS = 256
SEQ_UNROLL = 8
COL_Q, COL_K, COL_V, COL_O, COL_ZA, COL_U, COL_VB, COL_ZB, COL_GA, COL_GB = range(10)

NT_DIMS = (((1,), (1,)), ((), ()))
TN_DIMS = (((0,), (0,)), ((), ()))

VMEM_LIMIT_BYTES = 56 * 1024 * 1024


def _sds(shape, dtype):
    return jax.ShapeDtypeStruct(shape, dtype)


def _iota(shape, dim):
    return lax.broadcasted_iota(jnp.int32, shape, dim)


def _dot(a, b):
    return jnp.dot(a, b, preferred_element_type=F32)


def _split_bf16(x):
    hi = x.astype(BF16)
    lo = (x - hi.astype(F32)).astype(BF16)
    return hi, lo


def _mask_dot_left(mask_b, x):
    hi, lo = _split_bf16(x)
    return _dot(mask_b, hi) + _dot(mask_b, lo)


def _mask_dot_right(x, mask_b):
    hi, lo = _split_bf16(x)
    return _dot(hi, mask_b) + _dot(lo, mask_b)


def _log_sigmoid(x):
    return -(jnp.maximum(-x, 0.0) + jnp.log1p(jnp.exp(-jnp.abs(x))))


def _sigmoid(x):
    return 0.5 * jnp.tanh(0.5 * x) + 0.5


def _silu(x):
    half = 0.5 * x
    return half + half * jnp.tanh(half)


def _segment_masks(seg_len):
    r = _iota((BLOCK_ROWS, BLOCK_ROWS), 0)
    c = _iota((BLOCK_ROWS, BLOCK_ROWS), 1)
    shift = seg_len.bit_length() - 1
    same = (r >> shift) == (c >> shift)
    causal = jnp.logical_and(same, c <= r)
    causal_t = jnp.logical_and(same, r <= c)
    return same, causal, causal_t


def _as_bf16_mask(m):
    return jnp.where(m, 1.0, 0.0).astype(BF16)


def _norm_kernel(xp_ref, xs_ref, g_ref, wg_ref, h_ref, gate_ref, *, n_prompt_blocks):
    def emit(x):
        ms = jnp.mean(x * x, axis=-1, keepdims=True)
        h = (x * lax.rsqrt(ms + EPS) * g_ref[...]).astype(BF16)
        h_ref[...] = h
        gate_ref[...] = lax.dot_general(h, wg_ref[...].astype(BF16), NT_DIMS,
                                        preferred_element_type=F32)

    i = pl.program_id(0)

    @pl.when(i < n_prompt_blocks)
    def _():
        emit(xp_ref[...])

    @pl.when(i >= n_prompt_blocks)
    def _():
        emit(xs_ref[...])


def _normalize(xp2d, xs2d, g_pre, w_in_t):
    tm = NORM_ROWS
    n_prompt_blocks = xp2d.shape[0] // tm
    nblk = n_prompt_blocks + xs2d.shape[0] // tm
    rows = nblk * tm
    return pl.pallas_call(
        functools.partial(_norm_kernel, n_prompt_blocks=n_prompt_blocks),
        grid=(nblk,),
        in_specs=[
            pl.BlockSpec((tm, D_MODEL), lambda i: (jnp.minimum(i, n_prompt_blocks - 1), 0)),
            pl.BlockSpec((tm, D_MODEL), lambda i: (jnp.maximum(i - n_prompt_blocks, 0), 0)),
            pl.BlockSpec((1, D_MODEL), lambda i: (0, 0)),
            pl.BlockSpec((LANES, D_MODEL), lambda i: (GATE_LO // LANES, 0)),
        ],
        out_specs=[
            pl.BlockSpec((tm, D_MODEL), lambda i: (i, 0)),
            pl.BlockSpec((tm, LANES), lambda i: (i, 0)),
        ],
        out_shape=[_sds((rows, D_MODEL), BF16), _sds((rows, LANES), F32)],
        compiler_params=pltpu.CompilerParams(
            dimension_semantics=("arbitrary",), vmem_limit_bytes=VMEM_LIMIT_BYTES),
        name="norm",
    )(xp2d, xs2d, g_pre, w_in_t)


def _proj_kernel(h_ref, wa_ref, wb_ref, o_ref, w_scr, *, tn, first_shifted):
    j = pl.program_id(0)

    def cast_rows(dst_lo, src_ref, src_lo, n):
        w_scr[dst_lo:dst_lo + n, :] = src_ref[src_lo:src_lo + n, :].astype(BF16)

    @pl.when(pl.program_id(1) == 0)
    def _cast():
        @pl.when(j < first_shifted)
        def _():
            for lo in range(0, tn, W_CAST_ROWS):
                cast_rows(lo, wa_ref, lo, W_CAST_ROWS)

        @pl.when(j >= first_shifted)
        def _():
            for lo in range(0, tn - W_CAST_ROWS, W_CAST_ROWS):
                cast_rows(lo, wa_ref, lo + N_GATE_COLS, W_CAST_ROWS)
            cast_rows(tn - W_CAST_ROWS, wa_ref, tn - W_CAST_ROWS + N_GATE_COLS,
                      W_CAST_ROWS - N_GATE_COLS)
            cast_rows(tn - N_GATE_COLS, wb_ref, 0, N_GATE_COLS)

    o_ref[...] = lax.dot_general(h_ref[...], w_scr[...], NT_DIMS,
                                 preferred_element_type=F32).astype(BF16)


def _project(h_all, w_in_t, tm, tn):
    rows = h_all.shape[0]
    return pl.pallas_call(
        functools.partial(_proj_kernel, tn=tn, first_shifted=GATE_LO // tn),
        grid=(N_MAIN // tn, rows // tm),
        in_specs=[
            pl.BlockSpec((tm, D_MODEL), lambda j, i: (i, 0)),
            pl.BlockSpec((tn, D_MODEL), lambda j, i: (j, 0)),
            pl.BlockSpec((N_GATE_COLS, D_MODEL), lambda j, i: ((j + 1) * (tn // N_GATE_COLS), 0)),
        ],
        out_specs=pl.BlockSpec((tm, tn), lambda j, i: (i, j)),
        out_shape=_sds((rows, N_MAIN), BF16),
        scratch_shapes=[pltpu.VMEM((tn, D_MODEL), BF16)],
        compiler_params=pltpu.CompilerParams(
            dimension_semantics=("arbitrary", "arbitrary"),
            vmem_limit_bytes=VMEM_LIMIT_BYTES),
        name="proj",
    )(h_all, w_in_t, w_in_t)


def _gate_prep(graw, gbias, causal_b, causal_t_b):
    g = graw + gbias
    bcum = _mask_dot_left(causal_b, _log_sigmoid(g))
    gt = g.T
    bcum_t = _mask_dot_right(_log_sigmoid(gt), causal_t_b)
    return g, bcum, gt, bcum_t


def _conv_taps(x, w, bias, prev_fn):
    acc = bias + w[CONV_W - 1:CONV_W] * x
    for d in range(1, CONV_W):
        xd = prev_fn(d, pltpu.roll(x, d, axis=0))
        acc = acc + w[CONV_W - 1 - d:CONV_W - d] * xd
    return _silu(acc)


def _cell_output(s_qk, vh_b, bt, bs, ig_row, mprev_col, causal, num_inter, den_inter):
    dmat = jnp.where(causal, bt - bs + ig_row, -jnp.inf)
    inter = bt + mprev_col
    m_row = jnp.maximum(jnp.max(dmat, axis=1, keepdims=True), inter)
    wts = jnp.exp(dmat - m_row)
    sc = jnp.exp(inter - m_row)
    s = s_qk * wts
    num = _dot(s.astype(BF16), vh_b) + sc * num_inter
    den = jnp.sum(s, axis=1, keepdims=True) + sc * den_inter
    inv = 1.0 / jnp.maximum(jnp.abs(den), jnp.exp(-m_row))
    return num * inv


def _gated_head_out(hcell, ghead_row, o_b, za_b):
    hn = hcell * lax.rsqrt(jnp.mean(hcell * hcell, axis=-1, keepdims=True) + EPS) * ghead_row
    return (_sigmoid(o_b.astype(F32)) * hn * _silu(za_b.astype(F32))).astype(BF16)


def _mlstm_prompt_kernel(qpre_ref, kpre_ref, qprev_ref, kprev_ref, v_ref, o_ref, za_ref, gate_ref,
                         gbias_ref, buf_ref, c0_ref, n0_ref, m0_ref, wconv_ref, bconv_ref, ghead_ref,
                         ya_ref, c_ref, n_ref, m_ref, buf_scr):
    chunk = pl.program_id(1)
    ntap = CONV_W - 1

    @pl.when(chunk == 0)
    def _init():
        c_ref[...] = c0_ref[...]
        n_ref[...] = n0_ref[...]
        m_ref[...] = m0_ref[...]
        buf_scr[...] = jnp.zeros_like(buf_scr)
        for d in range(1, CONV_W):
            buf_scr[d - 1, 0:d, :] = buf_ref[ntap - d:ntap, :]

    _, causal, causal_t = _segment_masks(BLOCK_ROWS)
    g, bcum, gt, bcum_t = _gate_prep(gate_ref[...], gbias_ref[...],
                                     _as_bf16_mask(causal), _as_bf16_mask(causal_t))

    r = _iota((ntap * BLOCK_ROWS, 2 * BLOCK_ROWS), 0)
    c = _iota((ntap * BLOCK_ROWS, 2 * BLOCK_ROWS), 1)
    src = BLOCK_ROWS + (r & (BLOCK_ROWS - 1)) - ((r >> (BLOCK_ROWS.bit_length() - 1)) + 1)
    first_col = jnp.where(chunk == 0, BLOCK_ROWS, 0)
    shift = jnp.where(jnp.logical_and(c == src, c >= first_col), 1.0, 0.0).astype(BF16)
    is_first = jnp.where(chunk == 0, 1.0, 0.0)

    for h in range(N_HEADS):
        cols = slice(h * D_HEAD, (h + 1) * D_HEAD)

        def conv_head(cur_ref, prev_ref, off):
            ccols = slice(off + h * D_HEAD, off + (h + 1) * D_HEAD)
            x_b = cur_ref[:, cols]
            shifted = _dot(shift, jnp.concatenate([prev_ref[:, cols], x_b], axis=0))
            w = wconv_ref[:, ccols]
            acc = bconv_ref[:, ccols] + w[ntap:CONV_W] * x_b.astype(F32)
            head = jnp.zeros((8, D_HEAD), F32)
            for d in range(1, CONV_W):
                acc = acc + w[ntap - d:CONV_W - d] * shifted[(d - 1) * BLOCK_ROWS:d * BLOCK_ROWS]
                head = head + w[ntap - d:CONV_W - d] * buf_scr[d - 1, :, ccols]
            acc = jnp.concatenate([acc[0:8] + is_first * head, acc[8:]], axis=0)
            return _silu(acc)

        qh = conv_head(qpre_ref, qprev_ref, 0)
        kh = conv_head(kpre_ref, kprev_ref, D_MODEL) * (D_HEAD ** -0.5)
        qb = qh.astype(BF16)
        kb = kh.astype(BF16)
        vh_b = v_ref[:, cols]

        bt = bcum[:, N_HEADS + h:N_HEADS + h + 1]
        bs = bcum_t[N_HEADS + h:N_HEADS + h + 1, :]
        ig_col = g[:, h:h + 1]
        ig_row = gt[h:h + 1, :]
        mprev = m_ref[h:h + 1, 0:1]
        c_old = c_ref[h]
        n_old = n_ref[h:h + 1, :]

        s_qk = lax.dot_general(qb, kb, NT_DIMS, preferred_element_type=F32)
        num_inter = lax.dot_general(qb, c_old.astype(BF16), NT_DIMS, preferred_element_type=F32)
        den_inter = jnp.sum(qh * n_old, axis=1, keepdims=True)
        hcell = _cell_output(s_qk, vh_b, bt, bs, ig_row, mprev, causal, num_inter, den_inter)
        ya_ref[:, cols] = _gated_head_out(hcell, ghead_ref[:, cols], o_ref[:, cols], za_ref[:, cols])

        b_last = bt[BLOCK_ROWS - 1:BLOCK_ROWS, :]
        src_row = b_last - bs + ig_row
        m_new = jnp.maximum(b_last + mprev, jnp.max(src_row, axis=1, keepdims=True))
        ws_col = jnp.exp(b_last - bt + ig_col - m_new)
        decay = jnp.exp(b_last + mprev - m_new)
        vw = (vh_b.astype(F32) * ws_col).astype(BF16)
        c_ref[h] = decay * c_old + lax.dot_general(vw, kb, TN_DIMS, preferred_element_type=F32)
        n_ref[h:h + 1, :] = decay * n_old + jnp.sum(ws_col * kh, axis=0, keepdims=True)
        m_ref[h:h + 1, :] = jnp.broadcast_to(m_new, (1, LANES))


def _mlstm_prompt(proj, gates, gbias, buf0, c0, n0, m0b, w_conv, b_conv, g_head_row, batch, seq):
    nchunk = seq // BLOCK_ROWS

    def stream(kind):
        return pl.BlockSpec((BLOCK_ROWS, D_MODEL), lambda b, c: (b * nchunk + c, kind))

    def prev_stream(kind):
        return pl.BlockSpec((BLOCK_ROWS, D_MODEL),
                            lambda b, c: (b * nchunk + jnp.maximum(c - 1, 0), kind))

    def per_batch(shape):
        nd = len(shape)
        return pl.BlockSpec((None,) + shape, lambda b, c: (b,) + (0,) * nd)

    def const(shape):
        nd = len(shape)
        return pl.BlockSpec(shape, lambda b, c: (0,) * nd)

    return pl.pallas_call(
        _mlstm_prompt_kernel,
        grid=(batch, nchunk),
        in_specs=[
            stream(COL_Q), stream(COL_K), prev_stream(COL_Q), prev_stream(COL_K),
            stream(COL_V), stream(COL_O), stream(COL_ZA),
            pl.BlockSpec((BLOCK_ROWS, LANES), lambda b, c: (b * nchunk + c, 0)),
            const((1, LANES)),
            per_batch((CONV_W - 1, 2 * D_MODEL)),
            per_batch((N_HEADS, D_HEAD, D_HEAD)),
            per_batch((N_HEADS, D_HEAD)),
            per_batch((N_HEADS, LANES)),
            const((CONV_W, 2 * D_MODEL)),
            const((1, 2 * D_MODEL)),
            const((1, D_MODEL)),
        ],
        out_specs=[
            pl.BlockSpec((None, BLOCK_ROWS, D_MODEL), lambda b, c: (b, c, 0)),
            per_batch((N_HEADS, D_HEAD, D_HEAD)),
            per_batch((N_HEADS, D_HEAD)),
            per_batch((N_HEADS, LANES)),
        ],
        out_shape=[
            _sds((batch, seq, D_MODEL), BF16),
            _sds((batch, N_HEADS, D_HEAD, D_HEAD), F32),
            _sds((batch, N_HEADS, D_HEAD), F32),
            _sds((batch, N_HEADS, LANES), F32),
        ],
        scratch_shapes=[pltpu.VMEM((CONV_W - 1, 8, 2 * D_MODEL), F32)],
        compiler_params=pltpu.CompilerParams(
            dimension_semantics=("arbitrary", "arbitrary"),
            vmem_limit_bytes=VMEM_LIMIT_BYTES),
        name="mlstm_prompt",
    )(proj, proj, proj, proj, proj, proj, proj, gates, gbias, buf0, c0, n0, m0b,
      w_conv, b_conv, g_head_row)


def _mlstm_decode_kernel(qpre_ref, kpre_ref, v_ref, o_ref, za_ref, gate_ref, gbias_ref,
                         histq_ref, histk_ref, c0_ref, n0_ref, m0_ref,
                         wq_ref, wk_ref, bq_ref, bk_ref, ghead_ref,
                         ya_ref, c_ref, n_ref, m_ref, rows_scr, dec_scr, *, seg_len):
    h = pl.program_id(1)
    nseg = BLOCK_ROWS // seg_len
    seg_shift = seg_len.bit_length() - 1
    same, causal, causal_t = _segment_masks(seg_len)
    same_b = _as_bf16_mask(same)
    g, bcum, gt, bcum_t = _gate_prep(gate_ref[...], gbias_ref[...],
                                     _as_bf16_mask(causal), _as_bf16_mask(causal_t))
    lf = _log_sigmoid(g)
    btot = _mask_dot_left(same_b, lf)
    btot_t = _mask_dot_right(_log_sigmoid(gt), same_b)
    rows_scr[0] = gt
    rows_scr[1] = bcum_t
    rows_scr[2] = btot_t
    ig_row = rows_scr[0, pl.ds(h, 1), :]
    bs = rows_scr[1, pl.ds(N_HEADS + h, 1), :]
    blast_row = rows_scr[2, pl.ds(N_HEADS + h, 1), :]

    lane = _iota((BLOCK_ROWS, LANES), 1)

    def pick_lane(x, idx):
        return jnp.sum(jnp.where(lane == idx, x, 0.0), axis=1, keepdims=True)

    ig_col = pick_lane(g, h)
    bt = pick_lane(bcum, N_HEADS + h)
    blast_col = pick_lane(btot, N_HEADS + h)
    m0_col = m0_ref[:, 0:1]

    t_idx = _iota((BLOCK_ROWS, D_HEAD), 0) & (seg_len - 1)

    def conv_head(pre_ref, hist_ref, w_ref, b_ref):
        hist = hist_ref[...]

        def prev_fn(d, xd):
            back = (BLOCK_ROWS - (CONV_W - 1 - d)) % BLOCK_ROWS
            hd = hist if back == 0 else pltpu.roll(hist, back, axis=0)
            return jnp.where(t_idx >= d, xd, hd)

        return _conv_taps(pre_ref[...].astype(F32), w_ref[...], b_ref[...], prev_fn)

    qh = conv_head(qpre_ref, histq_ref, wq_ref, bq_ref)
    kh = conv_head(kpre_ref, histk_ref, wk_ref, bk_ref) * (D_HEAD ** -0.5)
    qb = qh.astype(BF16)
    kb = kh.astype(BF16)
    vh_b = v_ref[...]

    src_col = blast_col - bt + ig_col
    src_row = blast_row - bs + ig_row
    seg_max = jnp.max(jnp.where(same, src_row, -jnp.inf), axis=1, keepdims=True)
    m_new_col = jnp.maximum(blast_col + m0_col, seg_max)
    ws_col = jnp.exp(src_col - m_new_col)
    decay_col = jnp.exp(blast_col + m0_col - m_new_col)
    dec_scr[...] = jnp.broadcast_to(decay_col, (BLOCK_ROWS, LANES))
    vwt = (vh_b.astype(F32) * ws_col).T

    row_seg = _iota((BLOCK_ROWS, D_HEAD), 0) >> seg_shift
    lane_seg = _iota((D_HEAD, BLOCK_ROWS), 1) >> seg_shift

    def per_sequence(b, acc):
        c_old = c0_ref[b, 0]
        q_b = jnp.where(row_seg == b, qh, 0.0).astype(BF16)
        acc = acc + lax.dot_general(q_b, c_old.astype(BF16), NT_DIMS, preferred_element_type=F32)
        vw_b = jnp.where(lane_seg == b, vwt, 0.0).astype(BF16)
        dec = dec_scr[pl.ds(b * seg_len, 1), 0:1]
        c_ref[b, 0] = dec * c_old + _dot(vw_b, kb)
        return acc

    num_inter = lax.fori_loop(0, nseg, per_sequence, jnp.zeros((BLOCK_ROWS, D_HEAD), F32),
                              unroll=SEQ_UNROLL)

    n_old = n0_ref[...]
    den_inter = jnp.sum(qh * n_old, axis=1, keepdims=True)
    s_qk = lax.dot_general(qb, kb, NT_DIMS, preferred_element_type=F32)
    hcell = _cell_output(s_qk, vh_b, bt, bs, ig_row, m0_col, causal, num_inter, den_inter)
    ya_ref[...] = _gated_head_out(hcell, ghead_ref[...], o_ref[...], za_ref[...])

    n_ref[...] = decay_col * n_old + _mask_dot_left(same_b, ws_col * kh)
    m_ref[...] = jnp.broadcast_to(m_new_col, (BLOCK_ROWS, LANES))


def _mlstm_decode(proj, gates, gbias, hist, c0, n0_rows, m0_rows, w_conv, b_conv, g_head_row,
                  seg_len, row0):
    rows = hist.shape[0]
    nblk = rows // BLOCK_ROWS
    nseg = BLOCK_ROWS // seg_len
    blk0 = row0 // BLOCK_ROWS

    def stream(kind):
        return pl.BlockSpec((BLOCK_ROWS, D_HEAD), lambda i, h: (blk0 + i, kind * N_HEADS + h))

    def head_cols(nrows, off):
        return pl.BlockSpec((nrows, D_HEAD), lambda i, h: (0, off * N_HEADS + h))

    row_head = pl.BlockSpec((BLOCK_ROWS, D_HEAD), lambda i, h: (i, h))
    state = pl.BlockSpec((nseg, 1, D_HEAD, D_HEAD), lambda i, h: (i, h, 0, 0))
    m_rows = pl.BlockSpec((None, BLOCK_ROWS, LANES), lambda i, h: (h, i, 0))

    return pl.pallas_call(
        functools.partial(_mlstm_decode_kernel, seg_len=seg_len),
        grid=(nblk, N_HEADS),
        in_specs=[
            stream(COL_Q), stream(COL_K), stream(COL_V), stream(COL_O), stream(COL_ZA),
            pl.BlockSpec((BLOCK_ROWS, LANES), lambda i, h: (blk0 + i, 0)),
            pl.BlockSpec((1, LANES), lambda i, h: (0, 0)),
            pl.BlockSpec((BLOCK_ROWS, D_HEAD), lambda i, h: (i, h)),
            pl.BlockSpec((BLOCK_ROWS, D_HEAD), lambda i, h: (i, N_HEADS + h)),
            state, row_head, m_rows,
            head_cols(CONV_W, 0), head_cols(CONV_W, 1), head_cols(1, 0), head_cols(1, 1),
            head_cols(1, 0),
        ],
        out_specs=[row_head, state, row_head, m_rows],
        out_shape=[
            _sds((rows, D_MODEL), BF16),
            _sds(c0.shape, F32),
            _sds((rows, D_MODEL), F32),
            _sds((N_HEADS, rows, LANES), F32),
        ],
        scratch_shapes=[pltpu.VMEM((3, BLOCK_ROWS, BLOCK_ROWS), F32),
                        pltpu.VMEM((BLOCK_ROWS, LANES), F32)],
        compiler_params=pltpu.CompilerParams(
            dimension_semantics=("arbitrary", "arbitrary"),
            vmem_limit_bytes=VMEM_LIMIT_BYTES),
        name="mlstm_decode",
    )(proj, proj, proj, proj, proj, gates, gbias, hist, hist, c0, n0_rows, m0_rows,
      w_conv, w_conv, b_conv, b_conv, g_head_row)


def _merge_kernel(ya_ref, u_ref, vb_ref, zb_ref, ga_ref, gb_ref, x_ref, lng_ref, lnb_ref,
                  ws_ref, bcol_ref, wpa_ref, wpb_ref, wout_ref, gpost_ref,
                  y_ref, vrows_ref, yb_scr, *, seg_len, vrows_every, tm):
    _, causal, _ = _segment_masks(seg_len)
    vb = vb_ref[...].astype(F32)
    mu = jnp.mean(vb, axis=-1, keepdims=True)
    xc = vb - mu
    vbn = xc * lax.rsqrt(jnp.mean(xc * xc, axis=-1, keepdims=True) + EPS) * lng_ref[...] + lnb_ref[...]
    vbn_b = vbn.astype(BF16)

    gcols = D_MODEL // N_GROUPS
    for grp in range(N_GROUPS):
        w_g = jnp.where(causal, ws_ref[grp], 0.0).astype(BF16)
        bias = bcol_ref[:, grp:grp + 1]
        cols = slice(grp * gcols, (grp + 1) * gcols)
        for blk in range(tm // BLOCK_ROWS):
            rows = slice(blk * BLOCK_ROWS, (blk + 1) * BLOCK_ROWS)
            s = _dot(w_g, vbn_b[rows, cols]) + bias
            yb = u_ref[rows, cols].astype(F32) * s * _silu(zb_ref[rows, cols].astype(F32))
            yb_scr[rows, cols] = yb.astype(BF16)

    pa = _dot(ya_ref[...], wpa_ref[...])
    pb = _dot(yb_scr[...], wpb_ref[...])
    merged = (_sigmoid(ga_ref[...].astype(F32)) * pa
              + _sigmoid(gb_ref[...].astype(F32)) * pb)
    out = _dot(merged.astype(BF16), wout_ref[...])
    post = out * lax.rsqrt(jnp.mean(out * out, axis=-1, keepdims=True) + EPS) * gpost_ref[...]
    y_ref[...] = x_ref[...] + post

    if vrows_every == 1:
        vrows_ref[...] = vbn
    else:
        @pl.when(pl.program_id(0) % vrows_every == vrows_every - 1)
        def _():
            vrows_ref[...] = vbn[tm - BLOCK_ROWS:, :]


def _merge(ya2d, proj, x2d, ln_g, ln_b, w_eff, bcol, w_pa, w_pb, w_out, g_post,
           seg_len, rows_per_seq, tm, row0):
    rows = x2d.shape[0]
    blk0 = row0 // tm
    if rows_per_seq:
        vrows_every = rows_per_seq // tm
        vrows_shape = (rows // rows_per_seq * BLOCK_ROWS, D_MODEL)
        vrows_spec = pl.BlockSpec((BLOCK_ROWS, D_MODEL), lambda i: (i // vrows_every, 0))
    else:
        vrows_every = 1
        vrows_shape = (rows, D_MODEL)
        vrows_spec = pl.BlockSpec((tm, D_MODEL), lambda i: (i, 0))

    def stream(kind):
        return pl.BlockSpec((tm, D_MODEL), lambda i: (blk0 + i, kind))

    def const(shape, single=False):
        nd = len(shape)
        if single:
            return pl.BlockSpec(shape, lambda i: (0,) * nd, pipeline_mode=pl.Buffered(1))
        return pl.BlockSpec(shape, lambda i: (0,) * nd)

    weight = const((D_MODEL, D_MODEL), single=True)
    return pl.pallas_call(
        functools.partial(_merge_kernel, seg_len=seg_len, vrows_every=vrows_every, tm=tm),
        grid=(rows // tm,),
        in_specs=[
            pl.BlockSpec((tm, D_MODEL), lambda i: (i, 0)),
            stream(COL_U), stream(COL_VB), stream(COL_ZB), stream(COL_GA), stream(COL_GB),
            pl.BlockSpec((tm, D_MODEL), lambda i: (i, 0)),
            const((1, D_MODEL)), const((1, D_MODEL)),
            const((N_GROUPS, BLOCK_ROWS, BLOCK_ROWS)), const((BLOCK_ROWS, LANES)),
            weight, weight, weight,
            const((1, D_MODEL)),
        ],
        out_specs=[pl.BlockSpec((tm, D_MODEL), lambda i: (i, 0)), vrows_spec],
        out_shape=[_sds((rows, D_MODEL), F32), _sds(vrows_shape, F32)],
        scratch_shapes=[pltpu.VMEM((tm, D_MODEL), BF16)],
        compiler_params=pltpu.CompilerParams(
            dimension_semantics=("arbitrary",),
            vmem_limit_bytes=VMEM_LIMIT_BYTES),
        name="merge",
    )(ya2d, proj, proj, proj, proj, proj, x2d, ln_g, ln_b, w_eff, bcol,
      w_pa, w_pb, w_out, g_post)


PROJ_ROWS_MAX = 1280
BF16_SUBLANES = 16


def _proj_rows(rows):
    best = BF16_SUBLANES
    for tm in range(BF16_SUBLANES, PROJ_ROWS_MAX + 1, BF16_SUBLANES):
        if rows % tm == 0:
            best = tm
    return best


def _pad_lanes(a):
    return jnp.pad(a, ((0, 0), (0, LANES - a.shape[1])))


def kernel(x_prompt, x_sample, state_mlstm_C, state_mlstm_n, state_mlstm_m, state_conv_qk,
           g_pre, w_in, b_ig, b_fg, w_conv, b_conv, g_head, ln_g, ln_b, w_s, b_s,
           w_pa, w_pb, w_out, g_post):
    batch, seq, _ = x_prompt.shape
    dec_batch, dec_seq, _ = x_sample.shape
    dec_rows = dec_batch * dec_seq
    assert seq % BLOCK_ROWS == 0 and BLOCK_ROWS % dec_seq == 0 and dec_rows % BLOCK_ROWS == 0
    assert dec_seq >= CONV_W - 1 and dec_seq & (dec_seq - 1) == 0

    gbias = _pad_lanes(jnp.concatenate([b_ig, b_fg])[None, :])
    g_pre2 = g_pre[None, :]
    b_conv2 = b_conv[None, :]
    g_head_row = g_head.reshape(1, D_MODEL)
    ln_g2, ln_b2, g_post2 = ln_g[None, :], ln_b[None, :], g_post[None, :]
    w_pa_b, w_pb_b, w_out_b = w_pa.astype(BF16), w_pb.astype(BF16), w_out.astype(BF16)

    p_rows = batch * seq
    xp2d = x_prompt.reshape(p_rows, D_MODEL)
    xs2d = x_sample.reshape(dec_rows, D_MODEL)
    assert p_rows % NORM_ROWS == 0 and dec_rows % NORM_ROWS == 0
    w_in_t = w_in.T
    h_all, gates = _normalize(xp2d, xs2d, g_pre2, w_in_t)
    all_rows = p_rows + dec_rows
    proj = _project(h_all, w_in_t, tm=_proj_rows(all_rows), tn=1024)

    zeros = functools.partial(jnp.zeros, dtype=F32)
    ya_p, c_p, n_p, m_pb = _mlstm_prompt(
        proj, gates, gbias,
        zeros((batch, CONV_W - 1, 2 * D_MODEL)), zeros((batch, N_HEADS, D_HEAD, D_HEAD)),
        zeros((batch, N_HEADS, D_HEAD)), zeros((batch, N_HEADS, LANES)),
        w_conv, b_conv2, g_head_row, batch, seq)
    y_p, vrows_p = _merge(ya_p.reshape(p_rows, D_MODEL), proj, xp2d, ln_g2, ln_b2,
                          w_s, _pad_lanes(b_s.T), w_pa_b, w_pb_b, w_out_b, g_post2,
                          seg_len=BLOCK_ROWS, rows_per_seq=seq, tm=256, row0=0)
    conv_p = jnp.stack([proj[(b + 1) * seq - (CONV_W - 1):(b + 1) * seq, :2 * D_MODEL]
                        for b in range(batch)]).astype(F32)

    proj_s = proj[p_rows:, :2 * D_MODEL]
    hist = jnp.pad(state_conv_qk, ((0, 0), (0, dec_seq - (CONV_W - 1)), (0, 0)))
    hist = hist.reshape(dec_rows, 2 * D_MODEL)
    n0_rows = jnp.repeat(state_mlstm_n.reshape(dec_batch, D_MODEL), dec_seq, axis=0)
    m0_rows = jnp.broadcast_to(jnp.repeat(state_mlstm_m.T, dec_seq, axis=1)[:, :, None],
                               (N_HEADS, dec_rows, LANES))
    ya_s, c_s, n_rows, m_rows = _mlstm_decode(
        proj, gates, gbias, hist, state_mlstm_C, n0_rows, m0_rows, w_conv, b_conv2, g_head_row,
        seg_len=dec_seq, row0=p_rows)
    nrep = BLOCK_ROWS // dec_seq
    w_eff_s = jnp.tile(w_s[:, :dec_seq, :dec_seq], (1, nrep, nrep))
    bcol_s = _pad_lanes(jnp.tile(b_s[:, :dec_seq], (1, nrep)).T)
    y_s, vrows_s = _merge(ya_s, proj, xs2d, ln_g2, ln_b2, w_eff_s, bcol_s,
                          w_pa_b, w_pb_b, w_out_b, g_post2,
                          seg_len=dec_seq, rows_per_seq=0, tm=256, row0=p_rows)
    n_s = n_rows[::dec_seq].reshape(dec_batch, N_HEADS, D_HEAD)
    m_s = m_rows[:, ::dec_seq, 0].T
    conv_s = proj_s.reshape(dec_batch, dec_seq, 2 * D_MODEL)
    conv_s = conv_s[:, dec_seq - (CONV_W - 1):].astype(F32)

    return (y_p.reshape(batch, seq, D_MODEL), y_s.reshape(dec_batch, dec_seq, D_MODEL),
            c_p, n_p, m_pb[:, :, 0], conv_p, vrows_p.reshape(batch, BLOCK_ROWS, D_MODEL),
            c_s, n_s, m_s, conv_s, vrows_s.reshape(dec_batch, dec_seq, D_MODEL))
```

```python
import functools

import jax
import jax.numpy as jnp
from jax import lax
from jax.experimental import pallas as pl
from jax.experimental.pallas import tpu as pltpu

F32 = jnp.float32
BF16 = jnp.bfloat16

D_MODEL = 2048
N_HEADS = 8
D_HEAD = 256
N_GROUPS = 8
CONV_W = 4
EPS = 1e-6
BLOCK_ROWS = 128
LANES = 128
N_MAIN = 10 * D_MODEL
GATE_LO = 5 * D_MODEL
N_GATE_COLS = 2 * N_HEADS
NORM_ROWS = 512
W_CAST_ROWS = 256
HEAD_GROUP = 8
SEQ_UNROLL = 8
COL_Q, COL_K, COL_V, COL_O, COL_ZA, COL_U, COL_VB, COL_ZB, COL_GA, COL_GB = range(10)

NT_DIMS = (((1,), (1,)), ((), ()))
TN_DIMS = (((0,), (0,)), ((), ()))

VMEM_LIMIT_BYTES = 56 * 1024 * 1024


def _sds(shape, dtype):
    return jax.ShapeDtypeStruct(shape, dtype)


def _iota(shape, dim):
    return lax.broadcasted_iota(jnp.int32, shape, dim)


def _dot(a, b):
    return jnp.dot(a, b, preferred_element_type=F32)


def _split_bf16(x):
    hi = x.astype(BF16)
    lo = (x - hi.astype(F32)).astype(BF16)
    return hi, lo


def _mask_dot_left(mask_b, x):
    hi, lo = _split_bf16(x)
    return _dot(mask_b, hi) + _dot(mask_b, lo)


def _mask_dot_right(x, mask_b):
    hi, lo = _split_bf16(x)
    return _dot(hi, mask_b) + _dot(lo, mask_b)


def _log_sigmoid(x):
    return -(jnp.maximum(-x, 0.0) + jnp.log1p(jnp.exp(-jnp.abs(x))))


def _sigmoid(x):
    return 0.5 * jnp.tanh(0.5 * x) + 0.5


def _silu(x):
    half = 0.5 * x
    return half + half * jnp.tanh(half)


def _segment_masks(seg_len):
    r = _iota((BLOCK_ROWS, BLOCK_ROWS), 0)
    c = _iota((BLOCK_ROWS, BLOCK_ROWS), 1)
    shift = seg_len.bit_length() - 1
    same = (r >> shift) == (c >> shift)
    causal = jnp.logical_and(same, c <= r)
    causal_t = jnp.logical_and(same, r <= c)
    return same, causal, causal_t


def _as_bf16_mask(m):
    return jnp.where(m, 1.0, 0.0).astype(BF16)


def _norm_kernel(xp_ref, xs_ref, g_ref, wg_ref, h_ref, gate_ref, *, n_prompt_blocks):
    def emit(x):
        ms = jnp.mean(x * x, axis=-1, keepdims=True)
        h = (x * lax.rsqrt(ms + EPS) * g_ref[...]).astype(BF16)
        h_ref[...] = h
        gate_ref[...] = lax.dot_general(h, wg_ref[...].astype(BF16), NT_DIMS,
                                        preferred_element_type=F32)

    i = pl.program_id(0)

    @pl.when(i < n_prompt_blocks)
    def _():
        emit(xp_ref[...])

    @pl.when(i >= n_prompt_blocks)
    def _():
        emit(xs_ref[...])


def _normalize(xp2d, xs2d, g_pre, w_in_t):
    tm = NORM_ROWS
    n_prompt_blocks = xp2d.shape[0] // tm
    nblk = n_prompt_blocks + xs2d.shape[0] // tm
    rows = nblk * tm
    return pl.pallas_call(
        functools.partial(_norm_kernel, n_prompt_blocks=n_prompt_blocks),
        grid=(nblk,),
        in_specs=[
            pl.BlockSpec((tm, D_MODEL), lambda i: (jnp.minimum(i, n_prompt_blocks - 1), 0)),
            pl.BlockSpec((tm, D_MODEL), lambda i: (jnp.maximum(i - n_prompt_blocks, 0), 0)),
            pl.BlockSpec((1, D_MODEL), lambda i: (0, 0)),
            pl.BlockSpec((LANES, D_MODEL), lambda i: (GATE_LO // LANES, 0)),
        ],
        out_specs=[
            pl.BlockSpec((tm, D_MODEL), lambda i: (i, 0)),
            pl.BlockSpec((tm, LANES), lambda i: (i, 0)),
        ],
        out_shape=[_sds((rows, D_MODEL), BF16), _sds((rows, LANES), F32)],
        compiler_params=pltpu.CompilerParams(
            dimension_semantics=("arbitrary",), vmem_limit_bytes=VMEM_LIMIT_BYTES),
        name="norm",
    )(xp2d, xs2d, g_pre, w_in_t)


def _proj_kernel(h_ref, wa_ref, wb_ref, o_ref, w_scr, *, tn, first_shifted):
    j = pl.program_id(0)

    def cast_rows(dst_lo, src_ref, src_lo, n):
        w_scr[dst_lo:dst_lo + n, :] = src_ref[src_lo:src_lo + n, :].astype(BF16)

    @pl.when(pl.program_id(1) == 0)
    def _cast():
        @pl.when(j < first_shifted)
        def _():
            for lo in range(0, tn, W_CAST_ROWS):
                cast_rows(lo, wa_ref, lo, W_CAST_ROWS)

        @pl.when(j >= first_shifted)
        def _():
            for lo in range(0, tn - W_CAST_ROWS, W_CAST_ROWS):
                cast_rows(lo, wa_ref, lo + N_GATE_COLS, W_CAST_ROWS)
            cast_rows(tn - W_CAST_ROWS, wa_ref, tn - W_CAST_ROWS + N_GATE_COLS,
                      W_CAST_ROWS - N_GATE_COLS)
            cast_rows(tn - N_GATE_COLS, wb_ref, 0, N_GATE_COLS)

    o_ref[...] = lax.dot_general(h_ref[...], w_scr[...], NT_DIMS,
                                 preferred_element_type=F32).astype(BF16)


def _project(h_all, w_in_t, tm, tn):
    rows = h_all.shape[0]
    return pl.pallas_call(
        functools.partial(_proj_kernel, tn=tn, first_shifted=GATE_LO // tn),
        grid=(N_MAIN // tn, rows // tm),
        in_specs=[
            pl.BlockSpec((tm, D_MODEL), lambda j, i: (i, 0)),
            pl.BlockSpec((tn, D_MODEL), lambda j, i: (j, 0)),
            pl.BlockSpec((N_GATE_COLS, D_MODEL), lambda j, i: ((j + 1) * (tn // N_GATE_COLS), 0)),
        ],
        out_specs=pl.BlockSpec((tm, tn), lambda j, i: (i, j)),
        out_shape=_sds((rows, N_MAIN), BF16),
        scratch_shapes=[pltpu.VMEM((tn, D_MODEL), BF16)],
        compiler_params=pltpu.CompilerParams(
            dimension_semantics=("arbitrary", "arbitrary"),
            vmem_limit_bytes=VMEM_LIMIT_BYTES),
        name="proj",
    )(h_all, w_in_t, w_in_t)


def _gate_prep(graw, gbias, causal_b, causal_t_b):
    g = graw + gbias
    bcum = _mask_dot_left(causal_b, _log_sigmoid(g))
    gt = g.T
    bcum_t = _mask_dot_right(_log_sigmoid(gt), causal_t_b)
    return g, bcum, gt, bcum_t


def _conv_taps(x, w, bias, prev_fn):
    acc = bias + w[CONV_W - 1:CONV_W] * x
    for d in range(1, CONV_W):
        xd = prev_fn(d, pltpu.roll(x, d, axis=0))
        acc = acc + w[CONV_W - 1 - d:CONV_W - d] * xd
    return _silu(acc)


def _cell_output(s_qk, vh_b, bt, bs, ig_row, mprev_col, causal, num_inter, den_inter):
    dmat = jnp.where(causal, bt - bs + ig_row, -jnp.inf)
    inter = bt + mprev_col
    m_row = jnp.maximum(jnp.max(dmat, axis=1, keepdims=True), inter)
    wts = jnp.exp(dmat - m_row)
    sc = jnp.exp(inter - m_row)
    s = s_qk * wts
    num = _dot(s.astype(BF16), vh_b) + sc * num_inter
    den = jnp.sum(s, axis=1, keepdims=True) + sc * den_inter
    inv = 1.0 / jnp.maximum(jnp.abs(den), jnp.exp(-m_row))
    return num * inv


def _prefix_max_rows(x):
    rows = x.shape[0]
    row = _iota(x.shape, 0)
    k = 1
    while k < rows:
        x = jnp.where(row >= k, jnp.maximum(x, pltpu.roll(x, k, axis=0)), x)
        k *= 2
    return x


def _run_interleaved(stage_generators):
    live = list(stage_generators)
    while live:
        still = []
        for gen in live:
            try:
                next(gen)
                still.append(gen)
            except StopIteration:
                pass
        live = still


def _gated_head_out(hcell, ghead_row, o_b, za_b):
    hn = hcell * lax.rsqrt(jnp.mean(hcell * hcell, axis=-1, keepdims=True) + EPS) * ghead_row
    return (_sigmoid(o_b.astype(F32)) * hn * _silu(za_b.astype(F32))).astype(BF16)


def _mlstm_prompt_kernel(qpre_ref, kpre_ref, qprev_ref, kprev_ref, v_ref, o_ref, za_ref, gate_ref,
                         gbias_ref, buf_ref, c0_ref, n0_ref, m0_ref, wconv_ref, bconv_ref, ghead_ref,
                         ya_ref, c_ref, n_ref, m_ref, buf_scr):
    chunk = pl.program_id(1)
    ntap = CONV_W - 1

    @pl.when(chunk == 0)
    def _init():
        c_ref[...] = c0_ref[...]
        n_ref[...] = n0_ref[...]
        m_ref[...] = m0_ref[...]
        buf_scr[...] = jnp.zeros_like(buf_scr)
        for d in range(1, CONV_W):
            buf_scr[d - 1, 0:d, :] = buf_ref[ntap - d:ntap, :]

    _, causal, causal_t = _segment_masks(BLOCK_ROWS)
    g, bcum, gt, bcum_t = _gate_prep(gate_ref[...], gbias_ref[...],
                                     _as_bf16_mask(causal), _as_bf16_mask(causal_t))
    a_cols = pltpu.roll(g, N_HEADS, axis=1) - bcum
    a_cmax = _prefix_max_rows(a_cols)
    a_rows = gt[0:N_HEADS, :] - bcum_t[N_HEADS:2 * N_HEADS, :]
    ones_2k = jnp.ones((2 * BLOCK_ROWS, LANES), BF16)

    def both_halves(x):
        return jnp.concatenate([x, x], axis=1)

    r = _iota((ntap * BLOCK_ROWS, 2 * BLOCK_ROWS), 0)
    c = _iota((ntap * BLOCK_ROWS, 2 * BLOCK_ROWS), 1)
    src = BLOCK_ROWS + (r & (BLOCK_ROWS - 1)) - ((r >> (BLOCK_ROWS.bit_length() - 1)) + 1)
    first_col = jnp.where(chunk == 0, BLOCK_ROWS, 0)
    shift = jnp.where(jnp.logical_and(c == src, c >= first_col), 1.0, 0.0).astype(BF16)
    is_first = jnp.where(chunk == 0, 1.0, 0.0)

    def head_stages(h):
        cols = slice(h * D_HEAD, (h + 1) * D_HEAD)

        def shifted_rows(cur_ref, prev_ref):
            x_b = cur_ref[:, cols]
            return x_b, _dot(shift, jnp.concatenate([prev_ref[:, cols], x_b], axis=0))

        xq_b, sh_q = shifted_rows(qpre_ref, qprev_ref)
        xk_b, sh_k = shifted_rows(kpre_ref, kprev_ref)
        yield

        def conv_silu(x_b, shifted, off):
            ccols = slice(off + h * D_HEAD, off + (h + 1) * D_HEAD)
            w = wconv_ref[:, ccols]
            acc = bconv_ref[:, ccols] + w[ntap:CONV_W] * x_b.astype(F32)
            head = jnp.zeros((8, D_HEAD), F32)
            for d in range(1, CONV_W):
                acc = acc + w[ntap - d:CONV_W - d] * shifted[(d - 1) * BLOCK_ROWS:d * BLOCK_ROWS]
                head = head + w[ntap - d:CONV_W - d] * buf_scr[d - 1, :, ccols]
            acc = jnp.concatenate([acc[0:8] + is_first * head, acc[8:]], axis=0)
            return _silu(acc)

        qh = conv_silu(xq_b, sh_q, 0)
        kh = conv_silu(xk_b, sh_k, D_MODEL) * (D_HEAD ** -0.5)
        qb = qh.astype(BF16)
        kb = kh.astype(BF16)
        s_qk = lax.dot_general(qb, kb, NT_DIMS, preferred_element_type=F32)
        num_inter = lax.dot_general(qb, c_ref[h].astype(BF16), NT_DIMS, preferred_element_type=F32)
        yield

        lane_h = N_HEADS + h
        mprev = m_ref[h:h + 1, 0:1]
        n_old = n_ref[h:h + 1, :]
        vh_b = v_ref[:, cols]
        bt = jnp.broadcast_to(bcum[:, lane_h:lane_h + 1], (BLOCK_ROWS, LANES))
        c_row = jnp.maximum(jnp.broadcast_to(a_cmax[:, lane_h:lane_h + 1], (BLOCK_ROWS, LANES)), mprev)
        a_row = a_rows[h:h + 1, :]
        s = s_qk * jnp.exp(jnp.where(causal, a_row - c_row, -jnp.inf))
        s_hi, s_lo = _split_bf16(s)
        num_intra = _dot(s_hi, vh_b)
        s_sum = _dot(jnp.concatenate([s_hi, s_lo], axis=1), ones_2k)
        q_dot_n = lax.dot_general(qb, jnp.broadcast_to(n_old.astype(BF16), (BLOCK_ROWS, D_HEAD)),
                                  NT_DIMS, preferred_element_type=F32)

        b_last = bcum[BLOCK_ROWS - 1:BLOCK_ROWS, lane_h:lane_h + 1]
        src_row = b_last + a_row
        m_new = jnp.maximum(b_last + mprev, jnp.max(src_row, axis=1, keepdims=True))
        ws_row = jnp.exp(src_row - m_new)
        decay = jnp.exp(b_last + mprev - m_new)
        vw_t = (vh_b.T.astype(F32) * ws_row).astype(BF16)
        c_upd = _dot(vw_t, kb)
        n_upd = _dot(jnp.broadcast_to(ws_row.astype(BF16), (8, BLOCK_ROWS)), kb)[0:1, :]
        yield

        sc = jnp.exp(mprev - c_row)
        num = num_intra + both_halves(sc) * num_inter
        den = s_sum + sc * q_dot_n
        inv = 1.0 / jnp.maximum(jnp.abs(den), jnp.exp(-(bt + c_row)))
        msq = jnp.mean(num * num, axis=-1, keepdims=True)
        row_scale = inv * lax.rsqrt(inv * inv * msq + EPS)
        hn = num * both_halves(row_scale) * ghead_ref[:, cols]
        ya_ref[:, cols] = (_sigmoid(o_ref[:, cols].astype(F32)) * hn
                           * _silu(za_ref[:, cols].astype(F32))).astype(BF16)
        c_ref[h] = decay * c_ref[h] + c_upd
        n_ref[h:h + 1, :] = decay * n_old + n_upd
        m_ref[h:h + 1, :] = jnp.broadcast_to(m_new, (1, LANES))

    for h0 in range(0, N_HEADS, HEAD_GROUP):
        _run_interleaved([head_stages(h) for h in range(h0, h0 + HEAD_GROUP)])


def _mlstm_prompt(proj, gates, gbias, buf0, c0, n0, m0b, w_conv, b_conv, g_head_row, batch, seq):
    nchunk = seq // BLOCK_ROWS

    def stream(kind):
        return pl.BlockSpec((BLOCK_ROWS, D_MODEL), lambda b, c: (b * nchunk + c, kind))

    def prev_stream(kind):
        return pl.BlockSpec((BLOCK_ROWS, D_MODEL),
                            lambda b, c: (b * nchunk + jnp.maximum(c - 1, 0), kind))

    def per_batch(shape):
        nd = len(shape)
        return pl.BlockSpec((None,) + shape, lambda b, c: (b,) + (0,) * nd)

    def const(shape):
        nd = len(shape)
        return pl.BlockSpec(shape, lambda b, c: (0,) * nd)

    return pl.pallas_call(
        _mlstm_prompt_kernel,
        grid=(batch, nchunk),
        in_specs=[
            stream(COL_Q), stream(COL_K), prev_stream(COL_Q), prev_stream(COL_K),
            stream(COL_V), stream(COL_O), stream(COL_ZA),
            pl.BlockSpec((BLOCK_ROWS, LANES), lambda b, c: (b * nchunk + c, 0)),
            const((1, LANES)),
            per_batch((CONV_W - 1, 2 * D_MODEL)),
            per_batch((N_HEADS, D_HEAD, D_HEAD)),
            per_batch((N_HEADS, D_HEAD)),
            per_batch((N_HEADS, LANES)),
            const((CONV_W, 2 * D_MODEL)),
            const((1, 2 * D_MODEL)),
            const((1, D_MODEL)),
        ],
        out_specs=[
            pl.BlockSpec((None, BLOCK_ROWS, D_MODEL), lambda b, c: (b, c, 0)),
            per_batch((N_HEADS, D_HEAD, D_HEAD)),
            per_batch((N_HEADS, D_HEAD)),
            per_batch((N_HEADS, LANES)),
        ],
        out_shape=[
            _sds((batch, seq, D_MODEL), BF16),
            _sds((batch, N_HEADS, D_HEAD, D_HEAD), F32),
            _sds((batch, N_HEADS, D_HEAD), F32),
            _sds((batch, N_HEADS, LANES), F32),
        ],
        scratch_shapes=[pltpu.VMEM((CONV_W - 1, 8, 2 * D_MODEL), F32)],
        compiler_params=pltpu.CompilerParams(
            dimension_semantics=("arbitrary", "arbitrary"),
            vmem_limit_bytes=VMEM_LIMIT_BYTES),
        name="mlstm_prompt",
    )(proj, proj, proj, proj, proj, proj, proj, gates, gbias, buf0, c0, n0, m0b,
      w_conv, b_conv, g_head_row)


def _mlstm_decode_kernel(qpre_ref, kpre_ref, v_ref, o_ref, za_ref, gate_ref, gbias_ref,
                         histq_ref, histk_ref, c0_ref, n0_ref, m0_ref,
                         wq_ref, wk_ref, bq_ref, bk_ref, ghead_ref,
                         ya_ref, c_ref, n_ref, m_ref, rows_scr, dec_scr, *, seg_len):
    h = pl.program_id(1)
    nseg = BLOCK_ROWS // seg_len
    seg_shift = seg_len.bit_length() - 1
    same, causal, causal_t = _segment_masks(seg_len)
    same_b = _as_bf16_mask(same)
    g, bcum, gt, bcum_t = _gate_prep(gate_ref[...], gbias_ref[...],
                                     _as_bf16_mask(causal), _as_bf16_mask(causal_t))
    lf = _log_sigmoid(g)
    btot = _mask_dot_left(same_b, lf)
    btot_t = _mask_dot_right(_log_sigmoid(gt), same_b)
    rows_scr[0] = gt
    rows_scr[1] = bcum_t
    rows_scr[2] = btot_t
    ig_row = rows_scr[0, pl.ds(h, 1), :]
    bs = rows_scr[1, pl.ds(N_HEADS + h, 1), :]
    blast_row = rows_scr[2, pl.ds(N_HEADS + h, 1), :]

    lane = _iota((BLOCK_ROWS, LANES), 1)

    def pick_lane(x, idx):
        return jnp.sum(jnp.where(lane == idx, x, 0.0), axis=1, keepdims=True)

    ig_col = pick_lane(g, h)
    bt = pick_lane(bcum, N_HEADS + h)
    blast_col = pick_lane(btot, N_HEADS + h)
    m0_col = m0_ref[:, 0:1]

    t_idx = _iota((BLOCK_ROWS, D_HEAD), 0) & (seg_len - 1)

    def conv_head(pre_ref, hist_ref, w_ref, b_ref):
        hist = hist_ref[...]

        def prev_fn(d, xd):
            back = (BLOCK_ROWS - (CONV_W - 1 - d)) % BLOCK_ROWS
            hd = hist if back == 0 else pltpu.roll(hist, back, axis=0)
            return jnp.where(t_idx >= d, xd, hd)

        return _conv_taps(pre_ref[...].astype(F32), w_ref[...], b_ref[...], prev_fn)

    qh = conv_head(qpre_ref, histq_ref, wq_ref, bq_ref)
    kh = conv_head(kpre_ref, histk_ref, wk_ref, bk_ref) * (D_HEAD ** -0.5)
    qb = qh.astype(BF16)
    kb = kh.astype(BF16)
    vh_b = v_ref[...]

    src_col = blast_col - bt + ig_col
    src_row = blast_row - bs + ig_row
    seg_max = jnp.max(jnp.where(same, src_row, -jnp.inf), axis=1, keepdims=True)
    m_new_col = jnp.maximum(blast_col + m0_col, seg_max)
    ws_col = jnp.exp(src_col - m_new_col)
    decay_col = jnp.exp(blast_col + m0_col - m_new_col)
    dec_scr[...] = jnp.broadcast_to(decay_col, (BLOCK_ROWS, LANES))
    vwt = (vh_b.astype(F32) * ws_col).T

    row_seg = _iota((BLOCK_ROWS, D_HEAD), 0) >> seg_shift
    lane_seg = _iota((D_HEAD, BLOCK_ROWS), 1) >> seg_shift

    def per_sequence(b, acc):
        c_old = c0_ref[b, 0]
        q_b = jnp.where(row_seg == b, qh, 0.0).astype(BF16)
        acc = acc + lax.dot_general(q_b, c_old.astype(BF16), NT_DIMS, preferred_element_type=F32)
        vw_b = jnp.where(lane_seg == b, vwt, 0.0).astype(BF16)
        dec = dec_scr[pl.ds(b * seg_len, 1), 0:1]
        c_ref[b, 0] = dec * c_old + _dot(vw_b, kb)
        return acc

    num_inter = lax.fori_loop(0, nseg, per_sequence, jnp.zeros((BLOCK_ROWS, D_HEAD), F32),
                              unroll=SEQ_UNROLL)

    n_old = n0_ref[...]
    den_inter = jnp.sum(qh * n_old, axis=1, keepdims=True)
    s_qk = lax.dot_general(qb, kb, NT_DIMS, preferred_element_type=F32)
    hcell = _cell_output(s_qk, vh_b, bt, bs, ig_row, m0_col, causal, num_inter, den_inter)
    ya_ref[...] = _gated_head_out(hcell, ghead_ref[...], o_ref[...], za_ref[...])

    n_ref[...] = decay_col * n_old + _mask_dot_left(same_b, ws_col * kh)
    m_ref[...] = jnp.broadcast_to(m_new_col, (BLOCK_ROWS, LANES))


def _mlstm_decode(proj, gates, gbias, hist, c0, n0_rows, m0_rows, w_conv, b_conv, g_head_row,
                  seg_len, row0):
    rows = hist.shape[0]
    nblk = rows // BLOCK_ROWS
    nseg = BLOCK_ROWS // seg_len
    blk0 = row0 // BLOCK_ROWS

    def stream(kind):
        return pl.BlockSpec((BLOCK_ROWS, D_HEAD), lambda i, h: (blk0 + i, kind * N_HEADS + h))

    def head_cols(nrows, off):
        return pl.BlockSpec((nrows, D_HEAD), lambda i, h: (0, off * N_HEADS + h))

    row_head = pl.BlockSpec((BLOCK_ROWS, D_HEAD), lambda i, h: (i, h))
    state = pl.BlockSpec((nseg, 1, D_HEAD, D_HEAD), lambda i, h: (i, h, 0, 0))
    m_rows = pl.BlockSpec((None, BLOCK_ROWS, LANES), lambda i, h: (h, i, 0))

    return pl.pallas_call(
        functools.partial(_mlstm_decode_kernel, seg_len=seg_len),
        grid=(nblk, N_HEADS),
        in_specs=[
            stream(COL_Q), stream(COL_K), stream(COL_V), stream(COL_O), stream(COL_ZA),
            pl.BlockSpec((BLOCK_ROWS, LANES), lambda i, h: (blk0 + i, 0)),
            pl.BlockSpec((1, LANES), lambda i, h: (0, 0)),
            pl.BlockSpec((BLOCK_ROWS, D_HEAD), lambda i, h: (i, h)),
            pl.BlockSpec((BLOCK_ROWS, D_HEAD), lambda i, h: (i, N_HEADS + h)),
            state, row_head, m_rows,
            head_cols(CONV_W, 0), head_cols(CONV_W, 1), head_cols(1, 0), head_cols(1, 1),
            head_cols(1, 0),
        ],
        out_specs=[row_head, state, row_head, m_rows],
        out_shape=[
            _sds((rows, D_MODEL), BF16),
            _sds(c0.shape, F32),
            _sds((rows, D_MODEL), F32),
            _sds((N_HEADS, rows, LANES), F32),
        ],
        scratch_shapes=[pltpu.VMEM((3, BLOCK_ROWS, BLOCK_ROWS), F32),
                        pltpu.VMEM((BLOCK_ROWS, LANES), F32)],
        compiler_params=pltpu.CompilerParams(
            dimension_semantics=("arbitrary", "arbitrary"),
            vmem_limit_bytes=VMEM_LIMIT_BYTES),
        name="mlstm_decode",
    )(proj, proj, proj, proj, proj, gates, gbias, hist, hist, c0, n0_rows, m0_rows,
      w_conv, w_conv, b_conv, b_conv, g_head_row)


def _merge_kernel(ya_ref, u_ref, vb_ref, zb_ref, ga_ref, gb_ref, x_ref, lng_ref, lnb_ref,
                  ws_ref, bcol_ref, wpa_ref, wpb_ref, wout_ref, gpost_ref,
                  y_ref, vrows_ref, yb_scr, *, seg_len, vrows_every, tm):
    _, causal, _ = _segment_masks(seg_len)
    vb = vb_ref[...].astype(F32)
    mu = jnp.mean(vb, axis=-1, keepdims=True)
    xc = vb - mu
    vbn = xc * lax.rsqrt(jnp.mean(xc * xc, axis=-1, keepdims=True) + EPS) * lng_ref[...] + lnb_ref[...]
    vbn_b = vbn.astype(BF16)

    gcols = D_MODEL // N_GROUPS
    for grp in range(N_GROUPS):
        w_g = jnp.where(causal, ws_ref[grp], 0.0).astype(BF16)
        bias = bcol_ref[:, grp:grp + 1]
        cols = slice(grp * gcols, (grp + 1) * gcols)
        for blk in range(tm // BLOCK_ROWS):
            rows = slice(blk * BLOCK_ROWS, (blk + 1) * BLOCK_ROWS)
            s = _dot(w_g, vbn_b[rows, cols]) + bias
            yb = u_ref[rows, cols].astype(F32) * s * _silu(zb_ref[rows, cols].astype(F32))
            yb_scr[rows, cols] = yb.astype(BF16)

    pa = _dot(ya_ref[...], wpa_ref[...])
    pb = _dot(yb_scr[...], wpb_ref[...])
    merged = (_sigmoid(ga_ref[...].astype(F32)) * pa
              + _sigmoid(gb_ref[...].astype(F32)) * pb)
    out = _dot(merged.astype(BF16), wout_ref[...])
    post = out * lax.rsqrt(jnp.mean(out * out, axis=-1, keepdims=True) + EPS) * gpost_ref[...]
    y_ref[...] = x_ref[...] + post

    if vrows_every == 1:
        vrows_ref[...] = vbn
    else:
        @pl.when(pl.program_id(0) % vrows_every == vrows_every - 1)
        def _():
            vrows_ref[...] = vbn[tm - BLOCK_ROWS:, :]


def _merge(ya2d, proj, x2d, ln_g, ln_b, w_eff, bcol, w_pa, w_pb, w_out, g_post,
           seg_len, rows_per_seq, tm, row0):
    rows = x2d.shape[0]
    blk0 = row0 // tm
    if rows_per_seq:
        vrows_every = rows_per_seq // tm
        vrows_shape = (rows // rows_per_seq * BLOCK_ROWS, D_MODEL)
        vrows_spec = pl.BlockSpec((BLOCK_ROWS, D_MODEL), lambda i: (i // vrows_every, 0))
    else:
        vrows_every = 1
        vrows_shape = (rows, D_MODEL)
        vrows_spec = pl.BlockSpec((tm, D_MODEL), lambda i: (i, 0))

    def stream(kind):
        return pl.BlockSpec((tm, D_MODEL), lambda i: (blk0 + i, kind))

    def const(shape, single=False):
        nd = len(shape)
        if single:
            return pl.BlockSpec(shape, lambda i: (0,) * nd, pipeline_mode=pl.Buffered(1))
        return pl.BlockSpec(shape, lambda i: (0,) * nd)

    weight = const((D_MODEL, D_MODEL), single=True)
    return pl.pallas_call(
        functools.partial(_merge_kernel, seg_len=seg_len, vrows_every=vrows_every, tm=tm),
        grid=(rows // tm,),
        in_specs=[
            pl.BlockSpec((tm, D_MODEL), lambda i: (i, 0)),
            stream(COL_U), stream(COL_VB), stream(COL_ZB), stream(COL_GA), stream(COL_GB),
            pl.BlockSpec((tm, D_MODEL), lambda i: (i, 0)),
            const((1, D_MODEL)), const((1, D_MODEL)),
            const((N_GROUPS, BLOCK_ROWS, BLOCK_ROWS)), const((BLOCK_ROWS, LANES)),
            weight, weight, weight,
            const((1, D_MODEL)),
        ],
        out_specs=[pl.BlockSpec((tm, D_MODEL), lambda i: (i, 0)), vrows_spec],
        out_shape=[_sds((rows, D_MODEL), F32), _sds(vrows_shape, F32)],
        scratch_shapes=[pltpu.VMEM((tm, D_MODEL), BF16)],
        compiler_params=pltpu.CompilerParams(
            dimension_semantics=("arbitrary",),
            vmem_limit_bytes=VMEM_LIMIT_BYTES),
        name="merge",
    )(ya2d, proj, proj, proj, proj, proj, x2d, ln_g, ln_b, w_eff, bcol,
      w_pa, w_pb, w_out, g_post)


PROJ_ROWS_MAX = 1280
BF16_SUBLANES = 16


def _proj_rows(rows):
    best = BF16_SUBLANES
    for tm in range(BF16_SUBLANES, PROJ_ROWS_MAX + 1, BF16_SUBLANES):
        if rows % tm == 0:
            best = tm
    return best


def _pad_lanes(a):
    return jnp.pad(a, ((0, 0), (0, LANES - a.shape[1])))


def kernel(x_prompt, x_sample, state_mlstm_C, state_mlstm_n, state_mlstm_m, state_conv_qk,
           g_pre, w_in, b_ig, b_fg, w_conv, b_conv, g_head, ln_g, ln_b, w_s, b_s,
           w_pa, w_pb, w_out, g_post):
    batch, seq, _ = x_prompt.shape
    dec_batch, dec_seq, _ = x_sample.shape
    dec_rows = dec_batch * dec_seq
    assert seq % BLOCK_ROWS == 0 and BLOCK_ROWS % dec_seq == 0 and dec_rows % BLOCK_ROWS == 0
    assert dec_seq >= CONV_W - 1 and dec_seq & (dec_seq - 1) == 0

    gbias = _pad_lanes(jnp.concatenate([b_ig, b_fg])[None, :])
    g_pre2 = g_pre[None, :]
    b_conv2 = b_conv[None, :]
    g_head_row = g_head.reshape(1, D_MODEL)
    ln_g2, ln_b2, g_post2 = ln_g[None, :], ln_b[None, :], g_post[None, :]
    w_pa_b, w_pb_b, w_out_b = w_pa.astype(BF16), w_pb.astype(BF16), w_out.astype(BF16)

    p_rows = batch * seq
    xp2d = x_prompt.reshape(p_rows, D_MODEL)
    xs2d = x_sample.reshape(dec_rows, D_MODEL)
    assert p_rows % NORM_ROWS == 0 and dec_rows % NORM_ROWS == 0
    w_in_t = w_in.T
    h_all, gates = _normalize(xp2d, xs2d, g_pre2, w_in_t)
    all_rows = p_rows + dec_rows
    proj = _project(h_all, w_in_t, tm=_proj_rows(all_rows), tn=1024)

    zeros = functools.partial(jnp.zeros, dtype=F32)
    ya_p, c_p, n_p, m_pb = _mlstm_prompt(
        proj, gates, gbias,
        zeros((batch, CONV_W - 1, 2 * D_MODEL)), zeros((batch, N_HEADS, D_HEAD, D_HEAD)),
        zeros((batch, N_HEADS, D_HEAD)), zeros((batch, N_HEADS, LANES)),
        w_conv, b_conv2, g_head_row, batch, seq)
    y_p, vrows_p = _merge(ya_p.reshape(p_rows, D_MODEL), proj, xp2d, ln_g2, ln_b2,
                          w_s, _pad_lanes(b_s.T), w_pa_b, w_pb_b, w_out_b, g_post2,
                          seg_len=BLOCK_ROWS, rows_per_seq=seq, tm=256, row0=0)
    conv_p = jnp.stack([proj[(b + 1) * seq - (CONV_W - 1):(b + 1) * seq, :2 * D_MODEL]
                        for b in range(batch)]).astype(F32)

    proj_s = proj[p_rows:, :2 * D_MODEL]
    hist = jnp.pad(state_conv_qk, ((0, 0), (0, dec_seq - (CONV_W - 1)), (0, 0)))
    hist = hist.reshape(dec_rows, 2 * D_MODEL)
    n0_rows = jnp.repeat(state_mlstm_n.reshape(dec_batch, D_MODEL), dec_seq, axis=0)
    m0_rows = jnp.broadcast_to(jnp.repeat(state_mlstm_m.T, dec_seq, axis=1)[:, :, None],
                               (N_HEADS, dec_rows, LANES))
    ya_s, c_s, n_rows, m_rows = _mlstm_decode(
        proj, gates, gbias, hist, state_mlstm_C, n0_rows, m0_rows, w_conv, b_conv2, g_head_row,
        seg_len=dec_seq, row0=p_rows)
    nrep = BLOCK_ROWS // dec_seq
    w_eff_s = jnp.tile(w_s[:, :dec_seq, :dec_seq], (1, nrep, nrep))
    bcol_s = _pad_lanes(jnp.tile(b_s[:, :dec_seq], (1, nrep)).T)
    y_s, vrows_s = _merge(ya_s, proj, xs2d, ln_g2, ln_b2, w_eff_s, bcol_s,
                          w_pa_b, w_pb_b, w_out_b, g_post2,
                          seg_len=dec_seq, rows_per_seq=0, tm=256, row0=p_rows)
    n_s = n_rows[::dec_seq].reshape(dec_batch, N_HEADS, D_HEAD)
    m_s = m_rows[:, ::dec_seq, 0].T
    conv_s = proj_s.reshape(dec_batch, dec_seq, 2 * D_MODEL)
    conv_s = conv_s[:, dec_seq - (CONV_W - 1):].astype(F32)

    return (y_p.reshape(batch, seq, D_MODEL), y_s.reshape(dec_batch, dec_seq, D_MODEL),
            c_p, n_p, m_pb[:, :, 0], conv_p, vrows_p.reshape(batch, BLOCK_ROWS, D_MODEL),
            c_s, n_s, m_s, conv_s, vrows_s.reshape(dec_batch, dec_seq, D_MODEL))
```

```python
import functools
import math

import jax
import jax.numpy as jnp
from jax import lax
from jax.experimental import pallas as pl
from jax.experimental.pallas import tpu as pltpu

F32 = jnp.float32
BF16 = jnp.bfloat16

D_MODEL = 2048
N_HEADS = 8
D_HEAD = 256
N_GROUPS = 8
CONV_W = 4
EPS = 1e-6
BLOCK_ROWS = 128
LANES = 128
N_MAIN = 10 * D_MODEL
GATE_LO = 5 * D_MODEL
N_GATE_COLS = 2 * N_HEADS
NORM_ROWS = 512
W_CAST_ROWS = 256
HEAD_GROUP = 8
SEQ_UNROLL = 8
COL_Q, COL_K, COL_V, COL_O, COL_ZA, COL_U, COL_VB, COL_ZB, COL_GA, COL_GB = range(10)

NT_DIMS = (((1,), (1,)), ((), ()))

VMEM_LIMIT_BYTES = 56 * 1024 * 1024


def _sds(shape, dtype):
    return jax.ShapeDtypeStruct(shape, dtype)


def _iota(shape, dim):
    return lax.broadcasted_iota(jnp.int32, shape, dim)


def _dot(a, b):
    return jnp.dot(a, b, preferred_element_type=F32)


def _split_bf16(x):
    hi = x.astype(BF16)
    lo = (x - hi.astype(F32)).astype(BF16)
    return hi, lo


def _mask_dot_left(mask_b, x):
    hi, lo = _split_bf16(x)
    return _dot(mask_b, hi) + _dot(mask_b, lo)


def _mask_dot_right(x, mask_b):
    hi, lo = _split_bf16(x)
    return _dot(hi, mask_b) + _dot(lo, mask_b)


def _log_sigmoid(x):
    return -(jnp.maximum(-x, 0.0) + jnp.log1p(jnp.exp(-jnp.abs(x))))


def _sigmoid(x):
    return 0.5 * jnp.tanh(0.5 * x) + 0.5


def _silu(x):
    half = 0.5 * x
    return half + half * jnp.tanh(half)


def _segment_masks(seg_len):
    r = _iota((BLOCK_ROWS, BLOCK_ROWS), 0)
    c = _iota((BLOCK_ROWS, BLOCK_ROWS), 1)
    shift = seg_len.bit_length() - 1
    same = (r >> shift) == (c >> shift)
    causal = jnp.logical_and(same, c <= r)
    causal_t = jnp.logical_and(same, r <= c)
    return same, causal, causal_t


def _as_bf16_mask(m):
    return jnp.where(m, 1.0, 0.0).astype(BF16)


def _norm_kernel(xp_ref, xs_ref, g_ref, wg_ref, h_ref, gate_ref, *, n_prompt_blocks):
    def emit(x):
        ms = jnp.mean(x * x, axis=-1, keepdims=True)
        h = (x * lax.rsqrt(ms + EPS) * g_ref[...]).astype(BF16)
        h_ref[...] = h
        gate_ref[...] = lax.dot_general(h, wg_ref[...].astype(BF16), NT_DIMS,
                                        preferred_element_type=F32)

    i = pl.program_id(0)

    @pl.when(i < n_prompt_blocks)
    def _():
        emit(xp_ref[...])

    @pl.when(i >= n_prompt_blocks)
    def _():
        emit(xs_ref[...])


def _normalize(xp2d, xs2d, g_pre, w_in_t):
    tm = NORM_ROWS
    n_prompt_blocks = xp2d.shape[0] // tm
    nblk = n_prompt_blocks + xs2d.shape[0] // tm
    rows = nblk * tm
    return pl.pallas_call(
        functools.partial(_norm_kernel, n_prompt_blocks=n_prompt_blocks),
        grid=(nblk,),
        in_specs=[
            pl.BlockSpec((tm, D_MODEL), lambda i: (jnp.minimum(i, n_prompt_blocks - 1), 0)),
            pl.BlockSpec((tm, D_MODEL), lambda i: (jnp.maximum(i - n_prompt_blocks, 0), 0)),
            pl.BlockSpec((1, D_MODEL), lambda i: (0, 0)),
            pl.BlockSpec((LANES, D_MODEL), lambda i: (GATE_LO // LANES, 0)),
        ],
        out_specs=[
            pl.BlockSpec((tm, D_MODEL), lambda i: (i, 0)),
            pl.BlockSpec((tm, LANES), lambda i: (i, 0)),
        ],
        out_shape=[_sds((rows, D_MODEL), BF16), _sds((rows, LANES), F32)],
        compiler_params=pltpu.CompilerParams(
            dimension_semantics=("arbitrary",), vmem_limit_bytes=VMEM_LIMIT_BYTES),
        name="norm",
    )(xp2d, xs2d, g_pre, w_in_t)


def _cast_weight_block(wa_ref, wb_ref, w_scr, tn, shifted):
    def cast_rows(dst_lo, src_ref, src_lo, n):
        w_scr[dst_lo:dst_lo + n, :] = src_ref[src_lo:src_lo + n, :].astype(BF16)

    if not shifted:
        for lo in range(0, tn, W_CAST_ROWS):
            cast_rows(lo, wa_ref, lo, W_CAST_ROWS)
    else:
        for lo in range(0, tn - W_CAST_ROWS, W_CAST_ROWS):
            cast_rows(lo, wa_ref, lo + N_GATE_COLS, W_CAST_ROWS)
        cast_rows(tn - W_CAST_ROWS, wa_ref, tn - W_CAST_ROWS + N_GATE_COLS,
                  W_CAST_ROWS - N_GATE_COLS)
        cast_rows(tn - N_GATE_COLS, wb_ref, 0, N_GATE_COLS)


def _proj_kernel(h_ref, wa_ref, wb_ref, o_ref, w_scr, *, tn, shifted):
    @pl.when(pl.program_id(1) == 0)
    def _cast():
        _cast_weight_block(wa_ref, wb_ref, w_scr, tn, shifted)

    o_ref[...] = lax.dot_general(h_ref[...], w_scr[...], NT_DIMS,
                                 preferred_element_type=F32).astype(BF16)


def _proj_specs(tm, tn, col0):
    first = col0 // tn
    return ([pl.BlockSpec((tm, D_MODEL), lambda j, i: (i, 0)),
             pl.BlockSpec((tn, D_MODEL), lambda j, i: (first + j, 0)),
             pl.BlockSpec((N_GATE_COLS, D_MODEL),
                          lambda j, i: ((first + j + 1) * (tn // N_GATE_COLS), 0))],
            pl.BlockSpec((tm, tn), lambda j, i: (i, j)))


def _project(h_all, w_in_t, tm, tn, col0, ncols):
    rows = h_all.shape[0]
    in_specs, out_spec = _proj_specs(tm, tn, col0)
    return pl.pallas_call(
        functools.partial(_proj_kernel, tn=tn, shifted=col0 >= GATE_LO),
        grid=(ncols // tn, rows // tm),
        in_specs=in_specs,
        out_specs=out_spec,
        out_shape=_sds((rows, ncols), BF16),
        scratch_shapes=[pltpu.VMEM((tn, D_MODEL), BF16)],
        compiler_params=pltpu.CompilerParams(
            dimension_semantics=("arbitrary", "arbitrary"),
            vmem_limit_bytes=VMEM_LIMIT_BYTES),
        name="proj",
    )(h_all, w_in_t, w_in_t)


def _gate_prep(graw, gbias, causal_b, causal_t_b):
    g = graw + gbias
    bcum = _mask_dot_left(causal_b, _log_sigmoid(g))
    gt = g.T
    bcum_t = _mask_dot_right(_log_sigmoid(gt), causal_t_b)
    return g, bcum, gt, bcum_t


def _conv_taps(x, w, bias, prev_fn):
    acc = bias + w[CONV_W - 1:CONV_W] * x
    for d in range(1, CONV_W):
        xd = prev_fn(d, pltpu.roll(x, d, axis=0))
        acc = acc + w[CONV_W - 1 - d:CONV_W - d] * xd
    return _silu(acc)


def _cell_output(s_qk, vh_b, bt, bs, ig_row, mprev_col, causal, num_inter, den_inter):
    dmat = jnp.where(causal, bt - bs + ig_row, -jnp.inf)
    inter = bt + mprev_col
    m_row = jnp.maximum(jnp.max(dmat, axis=1, keepdims=True), inter)
    wts = jnp.exp(dmat - m_row)
    sc = jnp.exp(inter - m_row)
    s = s_qk * wts
    num = _dot(s.astype(BF16), vh_b) + sc * num_inter
    den = jnp.sum(s, axis=1, keepdims=True) + sc * den_inter
    inv = 1.0 / jnp.maximum(jnp.abs(den), jnp.exp(-m_row))
    return num * inv


def _run_interleaved(stage_generators, side=None, side_period=1):
    live = list(stage_generators)
    traced = 0
    while live:
        still = []
        for gen in live:
            if side is not None and traced % side_period == 0:
                next(side, None)
            traced += 1
            try:
                next(gen)
                still.append(gen)
            except StopIteration:
                pass
        live = still
    if side is not None:
        for _ in side:
            pass


def _gated_head_out(hcell, ghead_row, o_b, za_b):
    hn = hcell * lax.rsqrt(jnp.mean(hcell * hcell, axis=-1, keepdims=True) + EPS) * ghead_row
    return (_sigmoid(o_b.astype(F32)) * hn * _silu(za_b.astype(F32))).astype(BF16)


def _prefix_max_lanes(x):
    lanes = x.shape[1]
    lane = _iota(x.shape, 1)
    k = 1
    while k < lanes:
        x = jnp.where(lane >= k, jnp.maximum(x, pltpu.roll(x, k, axis=1)), x)
        k *= 2
    return x


def _mlstm_chunk(qpre_ref, kpre_ref, qprev_ref, kprev_ref, v_ref, o_ref, za_ref, gate_ref,
                 gbias_ref, shift_ref, wconv_ref, bconv_ref, ghead_ref, ya_ref, c_ref, n_ref, m_ref,
                 side=None, side_period=1):
    ntap = CONV_W - 1
    k_scale = D_HEAD ** -0.5

    _, causal, causal_t = _segment_masks(BLOCK_ROWS)
    gt = (gate_ref[...] + gbias_ref[...]).T
    bcum_t = _mask_dot_right(_log_sigmoid(gt[N_HEADS:2 * N_HEADS, :]), _as_bf16_mask(causal_t))
    a_rows = gt[0:N_HEADS, :] - bcum_t
    per_row = jnp.concatenate(
        [bcum_t, _prefix_max_lanes(a_rows),
         jnp.zeros((BLOCK_ROWS - 2 * N_HEADS, BLOCK_ROWS), F32)], axis=0).T
    ones_2k = jnp.ones((2 * BLOCK_ROWS, LANES), BF16)
    shift = shift_ref[...]

    def both_halves(x):
        return jnp.concatenate([x, x], axis=1)

    def lane_replicated(x, lane):
        return jnp.broadcast_to(x[:, lane:lane + 1], (BLOCK_ROWS, LANES))

    def head_stages(h):
        cols = slice(h * D_HEAD, (h + 1) * D_HEAD)

        def shifted_rows(cur_ref, prev_ref):
            x_b = cur_ref[:, cols]
            return x_b, _dot(shift, jnp.concatenate([prev_ref[:, cols], x_b], axis=0))

        xq_b, sh_q = shifted_rows(qpre_ref, qprev_ref)
        xk_b, sh_k = shifted_rows(kpre_ref, kprev_ref)
        yield

        def conv_silu(x_b, shifted, off):
            ccols = slice(off + h * D_HEAD, off + (h + 1) * D_HEAD)
            w = 0.5 * wconv_ref[:, ccols]
            half = 0.5 * bconv_ref[:, ccols] + w[ntap:CONV_W] * x_b.astype(F32)
            for d in range(1, CONV_W):
                half = half + w[ntap - d:CONV_W - d] * shifted[(d - 1) * BLOCK_ROWS:d * BLOCK_ROWS]
            return (half + half * jnp.tanh(half)).astype(BF16)

        qb = conv_silu(xq_b, sh_q, 0)
        kb = conv_silu(xk_b, sh_k, D_MODEL)
        s_qk = lax.dot_general(qb, kb, NT_DIMS, preferred_element_type=F32)
        num_inter = lax.dot_general(qb, c_ref[h].astype(BF16), NT_DIMS, preferred_element_type=F32)
        yield

        mprev = m_ref[h:h + 1, 0:1]
        n_old = n_ref[h:h + 1, :]
        vh_b = v_ref[:, cols]
        bt = lane_replicated(per_row, h)
        c_row = jnp.maximum(lane_replicated(per_row, N_HEADS + h), mprev)
        a_row = a_rows[h:h + 1, :]
        s = s_qk * jnp.exp(jnp.where(causal, a_row + math.log(k_scale) - c_row, -jnp.inf))
        s_hi, s_lo = _split_bf16(s)
        num_intra = _dot(s_hi, vh_b)
        s_sum = _dot(jnp.concatenate([s_hi, s_lo], axis=1), ones_2k)
        q_dot_n = lax.dot_general(qb, jnp.broadcast_to(n_old.astype(BF16), (BLOCK_ROWS, D_HEAD)),
                                  NT_DIMS, preferred_element_type=F32)

        b_last = bcum_t[h:h + 1, BLOCK_ROWS - 1:BLOCK_ROWS]
        src_row = b_last + a_row
        m_new = jnp.maximum(b_last + mprev, jnp.max(src_row, axis=1, keepdims=True))
        ws_row = jnp.exp(src_row - m_new) * k_scale
        decay = jnp.exp(b_last + mprev - m_new)
        vw_t = (vh_b.T.astype(F32) * ws_row).astype(BF16)
        c_upd = _dot(vw_t, kb)
        n_upd = _dot(jnp.broadcast_to(ws_row.astype(BF16), (8, BLOCK_ROWS)), kb)[0:1, :]
        yield

        sc = jnp.exp(mprev - c_row)
        num = num_intra + both_halves(sc) * num_inter
        den = s_sum + sc * q_dot_n
        inv = 1.0 / jnp.maximum(jnp.abs(den), jnp.exp(-(bt + c_row)))
        msq = jnp.mean(num * num, axis=-1, keepdims=True)
        row_scale = inv * lax.rsqrt(inv * inv * msq + EPS)
        hn = num * both_halves(row_scale) * ghead_ref[:, cols]
        ya_ref[:, cols] = (_sigmoid(o_ref[:, cols].astype(F32)) * hn
                           * _silu(za_ref[:, cols].astype(F32))).astype(BF16)
        c_ref[h] = decay * c_ref[h] + c_upd
        n_ref[h:h + 1, :] = decay * n_old + n_upd
        m_ref[h:h + 1, :] = jnp.broadcast_to(m_new, (1, LANES))

    assert HEAD_GROUP == N_HEADS
    _run_interleaved([head_stages(h) for h in range(N_HEADS)], side, side_period)


def _conv_shift_matrices():
    ntap = CONV_W - 1
    r = jnp.arange(ntap * BLOCK_ROWS)[:, None]
    c = jnp.arange(2 * BLOCK_ROWS)[None, :]
    src = BLOCK_ROWS + (r % BLOCK_ROWS) - (r // BLOCK_ROWS + 1)
    later = c == src
    first = jnp.logical_and(later, c >= BLOCK_ROWS)
    return jnp.stack([first, later]).astype(BF16)


N_MLSTM_INPUTS = 13
PROJ_SLAB_COLS = 256
PROJ_ROW_SLABS = 1
MLSTM_HEAD_STAGES = 4


def _proj_mlstm_kernel(h_ref, wa_ref, wb_ref, *refs, tn, n_row_blocks, n_chunks, chunks_per_seq):
    mlstm_in = refs[:N_MLSTM_INPUTS]
    o_ref, ya_ref, c_ref, n_ref, m_ref, w_scr = refs[N_MLSTM_INPUTS:]
    step = pl.program_id(0) * n_row_blocks + pl.program_id(1)

    @pl.when(pl.program_id(1) == 0)
    def _cast():
        _cast_weight_block(wa_ref, wb_ref, w_scr, tn, shifted=True)

    tm = h_ref.shape[0]
    slab_rows = tm // PROJ_ROW_SLABS
    n_slabs = PROJ_ROW_SLABS * (tn // PROJ_SLAB_COLS)

    def project_slabs():
        for lo in range(0, tn, PROJ_SLAB_COLS):
            for r0 in range(0, tm, slab_rows):
                o_ref[r0:r0 + slab_rows, lo:lo + PROJ_SLAB_COLS] = lax.dot_general(
                    h_ref[r0:r0 + slab_rows, :], w_scr[lo:lo + PROJ_SLAB_COLS, :], NT_DIMS,
                    preferred_element_type=F32).astype(BF16)
                yield

    @pl.when(step < n_chunks)
    def _with_chunk():
        @pl.when(step % chunks_per_seq == 0)
        def _new_sequence():
            c_ref[...] = jnp.zeros_like(c_ref)
            n_ref[...] = jnp.zeros_like(n_ref)
            m_ref[...] = jnp.zeros_like(m_ref)

        head_stage_count = N_HEADS * MLSTM_HEAD_STAGES
        _mlstm_chunk(*mlstm_in, ya_ref, c_ref, n_ref, m_ref, side=project_slabs(),
                     side_period=max(1, head_stage_count // n_slabs))

    @pl.when(step >= n_chunks)
    def _projection_only():
        _run_interleaved([project_slabs()])


def _project_with_prompt_mlstm(h_all, w_in_t, proj_a, gates, gbias, w_conv, b_conv, g_head_row,
                               tm, tn, col0, ncols, batch, seq):
    rows = h_all.shape[0]
    n_row_blocks = rows // tm
    cps = seq // BLOCK_ROWS
    n_chunks = batch * cps
    assert n_chunks <= (ncols // tn) * n_row_blocks and col0 >= GATE_LO
    in_specs, out_spec = _proj_specs(tm, tn, col0)

    def chunk_of(j, i):
        return jnp.minimum(j * n_row_blocks + i, n_chunks - 1)

    def stream(kind):
        return pl.BlockSpec((BLOCK_ROWS, D_MODEL), lambda j, i: (chunk_of(j, i), kind))

    def prev_stream(kind):
        def index(j, i):
            chunk = chunk_of(j, i)
            return (chunk - jnp.minimum(chunk % cps, 1), kind)
        return pl.BlockSpec((BLOCK_ROWS, D_MODEL), index)

    def per_seq(shape):
        nd = len(shape)
        return pl.BlockSpec((None,) + shape, lambda j, i: (chunk_of(j, i) // cps,) + (0,) * nd)

    def const(shape):
        nd = len(shape)
        return pl.BlockSpec(shape, lambda j, i: (0,) * nd)

    shift_rows = (CONV_W - 1) * BLOCK_ROWS
    mlstm_specs = [
        stream(COL_Q), stream(COL_K), prev_stream(COL_Q), prev_stream(COL_K),
        stream(COL_V), stream(COL_O), stream(COL_ZA),
        pl.BlockSpec((BLOCK_ROWS, LANES), lambda j, i: (chunk_of(j, i), 0)),
        const((1, LANES)),
        pl.BlockSpec((None, shift_rows, 2 * BLOCK_ROWS),
                     lambda j, i: (jnp.minimum(chunk_of(j, i) % cps, 1), 0, 0)),
        const((CONV_W, 2 * D_MODEL)),
        const((1, 2 * D_MODEL)),
        const((1, D_MODEL)),
    ]
    assert len(mlstm_specs) == N_MLSTM_INPUTS
    return pl.pallas_call(
        functools.partial(_proj_mlstm_kernel, tn=tn, n_row_blocks=n_row_blocks, n_chunks=n_chunks,
                          chunks_per_seq=cps),
        grid=(ncols // tn, n_row_blocks),
        in_specs=in_specs + mlstm_specs,
        out_specs=[
            out_spec,
            pl.BlockSpec((BLOCK_ROWS, D_MODEL), lambda j, i: (chunk_of(j, i), 0)),
            per_seq((N_HEADS, D_HEAD, D_HEAD)),
            per_seq((N_HEADS, D_HEAD)),
            per_seq((N_HEADS, LANES)),
        ],
        out_shape=[
            _sds((rows, ncols), BF16),
            _sds((batch * seq, D_MODEL), BF16),
            _sds((batch, N_HEADS, D_HEAD, D_HEAD), F32),
            _sds((batch, N_HEADS, D_HEAD), F32),
            _sds((batch, N_HEADS, LANES), F32),
        ],
        scratch_shapes=[pltpu.VMEM((tn, D_MODEL), BF16)],
        compiler_params=pltpu.CompilerParams(
            dimension_semantics=("arbitrary", "arbitrary"),
            vmem_limit_bytes=VMEM_LIMIT_BYTES),
        name="proj_mlstm",
    )(h_all, w_in_t, w_in_t, proj_a, proj_a, proj_a, proj_a, proj_a, proj_a, proj_a, gates, gbias,
      _conv_shift_matrices(), w_conv, b_conv, g_head_row)


def _mlstm_decode_kernel(qpre_ref, kpre_ref, v_ref, o_ref, za_ref, gate_ref, gbias_ref,
                         histq_ref, histk_ref, c0_ref, n0_ref, m0_ref,
                         wq_ref, wk_ref, bq_ref, bk_ref, ghead_ref,
                         ya_ref, c_ref, n_ref, m_ref, rows_scr, dec_scr, *, seg_len):
    h = pl.program_id(1)
    nseg = BLOCK_ROWS // seg_len
    seg_shift = seg_len.bit_length() - 1
    same, causal, causal_t = _segment_masks(seg_len)
    same_b = _as_bf16_mask(same)
    g, bcum, gt, bcum_t = _gate_prep(gate_ref[...], gbias_ref[...],
                                     _as_bf16_mask(causal), _as_bf16_mask(causal_t))
    lf = _log_sigmoid(g)
    btot = _mask_dot_left(same_b, lf)
    btot_t = _mask_dot_right(_log_sigmoid(gt), same_b)
    rows_scr[0] = gt
    rows_scr[1] = bcum_t
    rows_scr[2] = btot_t
    ig_row = rows_scr[0, pl.ds(h, 1), :]
    bs = rows_scr[1, pl.ds(N_HEADS + h, 1), :]
    blast_row = rows_scr[2, pl.ds(N_HEADS + h, 1), :]

    lane = _iota((BLOCK_ROWS, LANES), 1)

    def pick_lane(x, idx):
        return jnp.sum(jnp.where(lane == idx, x, 0.0), axis=1, keepdims=True)

    ig_col = pick_lane(g, h)
    bt = pick_lane(bcum, N_HEADS + h)
    blast_col = pick_lane(btot, N_HEADS + h)
    m0_col = m0_ref[:, 0:1]

    t_idx = _iota((BLOCK_ROWS, D_HEAD), 0) & (seg_len - 1)

    def conv_head(pre_ref, hist_ref, w_ref, b_ref):
        hist = hist_ref[...]

        def prev_fn(d, xd):
            back = (BLOCK_ROWS - (CONV_W - 1 - d)) % BLOCK_ROWS
            hd = hist if back == 0 else pltpu.roll(hist, back, axis=0)
            return jnp.where(t_idx >= d, xd, hd)

        return _conv_taps(pre_ref[...].astype(F32), w_ref[...], b_ref[...], prev_fn)

    qh = conv_head(qpre_ref, histq_ref, wq_ref, bq_ref)
    kh = conv_head(kpre_ref, histk_ref, wk_ref, bk_ref) * (D_HEAD ** -0.5)
    qb = qh.astype(BF16)
    kb = kh.astype(BF16)
    vh_b = v_ref[...]

    src_col = blast_col - bt + ig_col
    src_row = blast_row - bs + ig_row
    seg_max = jnp.max(jnp.where(same, src_row, -jnp.inf), axis=1, keepdims=True)
    m_new_col = jnp.maximum(blast_col + m0_col, seg_max)
    ws_col = jnp.exp(src_col - m_new_col)
    decay_col = jnp.exp(blast_col + m0_col - m_new_col)
    dec_scr[...] = jnp.broadcast_to(decay_col, (BLOCK_ROWS, LANES))
    vwt = (vh_b.astype(F32) * ws_col).T

    row_seg = _iota((BLOCK_ROWS, D_HEAD), 0) >> seg_shift
    lane_seg = _iota((D_HEAD, BLOCK_ROWS), 1) >> seg_shift

    def per_sequence(b, acc):
        c_old = c0_ref[b, 0]
        q_b = jnp.where(row_seg == b, qh, 0.0).astype(BF16)
        acc = acc + lax.dot_general(q_b, c_old.astype(BF16), NT_DIMS, preferred_element_type=F32)
        vw_b = jnp.where(lane_seg == b, vwt, 0.0).astype(BF16)
        dec = dec_scr[pl.ds(b * seg_len, 1), 0:1]
        c_ref[b, 0] = dec * c_old + _dot(vw_b, kb)
        return acc

    num_inter = lax.fori_loop(0, nseg, per_sequence, jnp.zeros((BLOCK_ROWS, D_HEAD), F32),
                              unroll=SEQ_UNROLL)

    n_old = n0_ref[...]
    den_inter = jnp.sum(qh * n_old, axis=1, keepdims=True)
    s_qk = lax.dot_general(qb, kb, NT_DIMS, preferred_element_type=F32)
    hcell = _cell_output(s_qk, vh_b, bt, bs, ig_row, m0_col, causal, num_inter, den_inter)
    ya_ref[...] = _gated_head_out(hcell, ghead_ref[...], o_ref[...], za_ref[...])

    n_ref[...] = decay_col * n_old + _mask_dot_left(same_b, ws_col * kh)
    m_ref[...] = jnp.broadcast_to(m_new_col, (BLOCK_ROWS, LANES))


def _mlstm_decode(proj, gates, gbias, hist, c0, n0_rows, m0_rows, w_conv, b_conv, g_head_row,
                  seg_len, row0):
    rows = hist.shape[0]
    nblk = rows // BLOCK_ROWS
    nseg = BLOCK_ROWS // seg_len
    blk0 = row0 // BLOCK_ROWS

    def stream(kind):
        return pl.BlockSpec((BLOCK_ROWS, D_HEAD), lambda i, h: (blk0 + i, kind * N_HEADS + h))

    def head_cols(nrows, off):
        return pl.BlockSpec((nrows, D_HEAD), lambda i, h: (0, off * N_HEADS + h))

    row_head = pl.BlockSpec((BLOCK_ROWS, D_HEAD), lambda i, h: (i, h))
    state = pl.BlockSpec((nseg, 1, D_HEAD, D_HEAD), lambda i, h: (i, h, 0, 0))
    m_rows = pl.BlockSpec((None, BLOCK_ROWS, LANES), lambda i, h: (h, i, 0))

    return pl.pallas_call(
        functools.partial(_mlstm_decode_kernel, seg_len=seg_len),
        grid=(nblk, N_HEADS),
        in_specs=[
            stream(COL_Q), stream(COL_K), stream(COL_V), stream(COL_O), stream(COL_ZA),
            pl.BlockSpec((BLOCK_ROWS, LANES), lambda i, h: (blk0 + i, 0)),
            pl.BlockSpec((1, LANES), lambda i, h: (0, 0)),
            pl.BlockSpec((BLOCK_ROWS, D_HEAD), lambda i, h: (i, h)),
            pl.BlockSpec((BLOCK_ROWS, D_HEAD), lambda i, h: (i, N_HEADS + h)),
            state, row_head, m_rows,
            head_cols(CONV_W, 0), head_cols(CONV_W, 1), head_cols(1, 0), head_cols(1, 1),
            head_cols(1, 0),
        ],
        out_specs=[row_head, state, row_head, m_rows],
        out_shape=[
            _sds((rows, D_MODEL), BF16),
            _sds(c0.shape, F32),
            _sds((rows, D_MODEL), F32),
            _sds((N_HEADS, rows, LANES), F32),
        ],
        scratch_shapes=[pltpu.VMEM((3, BLOCK_ROWS, BLOCK_ROWS), F32),
                        pltpu.VMEM((BLOCK_ROWS, LANES), F32)],
        compiler_params=pltpu.CompilerParams(
            dimension_semantics=("arbitrary", "arbitrary"),
            vmem_limit_bytes=VMEM_LIMIT_BYTES),
        name="mlstm_decode",
    )(proj, proj, proj, proj, proj, gates, gbias, hist, hist, c0, n0_rows, m0_rows,
      w_conv, w_conv, b_conv, b_conv, g_head_row)


def _merge_kernel(ya_ref, u_ref, vb_ref, zb_ref, ga_ref, gb_ref, x_ref, lng_ref, lnb_ref,
                  ws_ref, bcol_ref, wpa_ref, wpb_ref, wout_ref, gpost_ref,
                  y_ref, vrows_ref, yb_scr, *, seg_len, vrows_every, tm):
    _, causal, _ = _segment_masks(seg_len)
    vb = vb_ref[...].astype(F32)
    mu = jnp.mean(vb, axis=-1, keepdims=True)
    xc = vb - mu
    vbn = xc * lax.rsqrt(jnp.mean(xc * xc, axis=-1, keepdims=True) + EPS) * lng_ref[...] + lnb_ref[...]
    vbn_b = vbn.astype(BF16)

    gcols = D_MODEL // N_GROUPS
    for grp in range(N_GROUPS):
        w_g = jnp.where(causal, ws_ref[grp], 0.0).astype(BF16)
        bias = bcol_ref[:, grp:grp + 1]
        cols = slice(grp * gcols, (grp + 1) * gcols)
        for blk in range(tm // BLOCK_ROWS):
            rows = slice(blk * BLOCK_ROWS, (blk + 1) * BLOCK_ROWS)
            s = _dot(w_g, vbn_b[rows, cols]) + bias
            yb = u_ref[rows, cols].astype(F32) * s * _silu(zb_ref[rows, cols].astype(F32))
            yb_scr[rows, cols] = yb.astype(BF16)

    pa = _dot(ya_ref[...], wpa_ref[...])
    pb = _dot(yb_scr[...], wpb_ref[...])
    merged = (_sigmoid(ga_ref[...].astype(F32)) * pa
              + _sigmoid(gb_ref[...].astype(F32)) * pb)
    out = _dot(merged.astype(BF16), wout_ref[...])
    post = out * lax.rsqrt(jnp.mean(out * out, axis=-1, keepdims=True) + EPS) * gpost_ref[...]
    y_ref[...] = x_ref[...] + post

    if vrows_every == 1:
        vrows_ref[...] = vbn
    else:
        @pl.when(pl.program_id(0) % vrows_every == vrows_every - 1)
        def _():
            vrows_ref[...] = vbn[tm - BLOCK_ROWS:, :]


def _merge(ya2d, proj, x2d, ln_g, ln_b, w_eff, bcol, w_pa, w_pb, w_out, g_post,
           seg_len, rows_per_seq, tm, row0):
    rows = x2d.shape[0]
    blk0 = row0 // tm
    if rows_per_seq:
        vrows_every = rows_per_seq // tm
        vrows_shape = (rows // rows_per_seq * BLOCK_ROWS, D_MODEL)
        vrows_spec = pl.BlockSpec((BLOCK_ROWS, D_MODEL), lambda i: (i // vrows_every, 0))
    else:
        vrows_every = 1
        vrows_shape = (rows, D_MODEL)
        vrows_spec = pl.BlockSpec((tm, D_MODEL), lambda i: (i, 0))

    def stream(kind):
        return pl.BlockSpec((tm, D_MODEL), lambda i: (blk0 + i, kind - COL_U))

    def const(shape, single=False):
        nd = len(shape)
        if single:
            return pl.BlockSpec(shape, lambda i: (0,) * nd, pipeline_mode=pl.Buffered(1))
        return pl.BlockSpec(shape, lambda i: (0,) * nd)

    weight = const((D_MODEL, D_MODEL), single=True)
    return pl.pallas_call(
        functools.partial(_merge_kernel, seg_len=seg_len, vrows_every=vrows_every, tm=tm),
        grid=(rows // tm,),
        in_specs=[
            pl.BlockSpec((tm, D_MODEL), lambda i: (i, 0)),
            stream(COL_U), stream(COL_VB), stream(COL_ZB), stream(COL_GA), stream(COL_GB),
            pl.BlockSpec((tm, D_MODEL), lambda i: (i, 0)),
            const((1, D_MODEL)), const((1, D_MODEL)),
            const((N_GROUPS, BLOCK_ROWS, BLOCK_ROWS)), const((BLOCK_ROWS, LANES)),
            weight, weight, weight,
            const((1, D_MODEL)),
        ],
        out_specs=[pl.BlockSpec((tm, D_MODEL), lambda i: (i, 0)), vrows_spec],
        out_shape=[_sds((rows, D_MODEL), F32), _sds(vrows_shape, F32)],
        scratch_shapes=[pltpu.VMEM((tm, D_MODEL), BF16)],
        compiler_params=pltpu.CompilerParams(
            dimension_semantics=("arbitrary",),
            vmem_limit_bytes=VMEM_LIMIT_BYTES),
        name="merge",
    )(ya2d, proj, proj, proj, proj, proj, x2d, ln_g, ln_b, w_eff, bcol,
      w_pa, w_pb, w_out, g_post)


PROJ_COLS = 1024
PROJ_ROWS_MAX = 1280
BF16_SUBLANES = 16


def _proj_rows(rows):
    best = BF16_SUBLANES
    for tm in range(BF16_SUBLANES, PROJ_ROWS_MAX + 1, BF16_SUBLANES):
        if rows % tm == 0:
            best = tm
    return best


def _pad_lanes(a):
    return jnp.pad(a, ((0, 0), (0, LANES - a.shape[1])))


def kernel(x_prompt, x_sample, state_mlstm_C, state_mlstm_n, state_mlstm_m, state_conv_qk,
           g_pre, w_in, b_ig, b_fg, w_conv, b_conv, g_head, ln_g, ln_b, w_s, b_s,
           w_pa, w_pb, w_out, g_post):
    batch, seq, _ = x_prompt.shape
    dec_batch, dec_seq, _ = x_sample.shape
    dec_rows = dec_batch * dec_seq
    assert seq % BLOCK_ROWS == 0 and BLOCK_ROWS % dec_seq == 0 and dec_rows % BLOCK_ROWS == 0
    assert dec_seq >= CONV_W - 1 and dec_seq & (dec_seq - 1) == 0

    gbias = _pad_lanes(jnp.concatenate([b_ig, b_fg])[None, :])
    g_pre2 = g_pre[None, :]
    b_conv2 = b_conv[None, :]
    g_head_row = g_head.reshape(1, D_MODEL)
    ln_g2, ln_b2, g_post2 = ln_g[None, :], ln_b[None, :], g_post[None, :]
    w_pa_b, w_pb_b, w_out_b = w_pa.astype(BF16), w_pb.astype(BF16), w_out.astype(BF16)

    p_rows = batch * seq
    xp2d = x_prompt.reshape(p_rows, D_MODEL)
    xs2d = x_sample.reshape(dec_rows, D_MODEL)
    assert p_rows % NORM_ROWS == 0 and dec_rows % NORM_ROWS == 0
    w_in_t = w_in.T
    h_all, gates = _normalize(xp2d, xs2d, g_pre2, w_in_t)
    tm = _proj_rows(p_rows + dec_rows)
    proj_a = _project(h_all, w_in_t, tm=tm, tn=PROJ_COLS, col0=0, ncols=GATE_LO)
    proj_b, ya_p, c_p, n_p, m_pb = _project_with_prompt_mlstm(
        h_all, w_in_t, proj_a, gates, gbias, w_conv, b_conv2, g_head_row,
        tm=tm, tn=PROJ_COLS, col0=GATE_LO, ncols=N_MAIN - GATE_LO, batch=batch, seq=seq)

    y_p, vrows_p = _merge(ya_p, proj_b, xp2d, ln_g2, ln_b2,
                          w_s, _pad_lanes(b_s.T), w_pa_b, w_pb_b, w_out_b, g_post2,
                          seg_len=BLOCK_ROWS, rows_per_seq=seq, tm=256, row0=0)
    conv_p = jnp.stack([proj_a[(b + 1) * seq - (CONV_W - 1):(b + 1) * seq, :2 * D_MODEL]
                        for b in range(batch)]).astype(F32)

    proj_s = proj_a[p_rows:, :2 * D_MODEL]
    hist = jnp.pad(state_conv_qk, ((0, 0), (0, dec_seq - (CONV_W - 1)), (0, 0)))
    hist = hist.reshape(dec_rows, 2 * D_MODEL)
    n0_rows = jnp.repeat(state_mlstm_n.reshape(dec_batch, D_MODEL), dec_seq, axis=0)
    m0_rows = jnp.broadcast_to(jnp.repeat(state_mlstm_m.T, dec_seq, axis=1)[:, :, None],
                               (N_HEADS, dec_rows, LANES))
    ya_s, c_s, n_rows, m_rows = _mlstm_decode(
        proj_a, gates, gbias, hist, state_mlstm_C, n0_rows, m0_rows, w_conv, b_conv2, g_head_row,
        seg_len=dec_seq, row0=p_rows)
    nrep = BLOCK_ROWS // dec_seq
    w_eff_s = jnp.tile(w_s[:, :dec_seq, :dec_seq], (1, nrep, nrep))
    bcol_s = _pad_lanes(jnp.tile(b_s[:, :dec_seq], (1, nrep)).T)
    y_s, vrows_s = _merge(ya_s, proj_b, xs2d, ln_g2, ln_b2, w_eff_s, bcol_s,
                          w_pa_b, w_pb_b, w_out_b, g_post2,
                          seg_len=dec_seq, rows_per_seq=0, tm=256, row0=p_rows)
    n_s = n_rows[::dec_seq].reshape(dec_batch, N_HEADS, D_HEAD)
    m_s = m_rows[:, ::dec_seq, 0].T
    conv_s = proj_s.reshape(dec_batch, dec_seq, 2 * D_MODEL)
    conv_s = conv_s[:, dec_seq - (CONV_W - 1):].astype(F32)

    return (y_p.reshape(batch, seq, D_MODEL), y_s.reshape(dec_batch, dec_seq, D_MODEL),
            c_p, n_p, m_pb[:, :, 0], conv_p, vrows_p.reshape(batch, BLOCK_ROWS, D_MODEL),
            c_s, n_s, m_s, conv_s, vrows_s.reshape(dec_batch, dec_seq, D_MODEL))
```

```python
import functools
import math

import jax
import jax.numpy as jnp
from jax import lax
from jax.experimental import pallas as pl
from jax.experimental.pallas import tpu as pltpu

F32 = jnp.float32
BF16 = jnp.bfloat16

D_MODEL = 2048
N_HEADS = 8
D_HEAD = 256
N_GROUPS = 8
CONV_W = 4
EPS = 1e-6
BLOCK_ROWS = 128
LANES = 128
N_MAIN = 10 * D_MODEL
GATE_LO = 5 * D_MODEL
N_GATE_COLS = 2 * N_HEADS
NORM_ROWS = 512
W_CAST_ROWS = 256
SIDE_CAST_ROWS = 32
HEAD_GROUP = 2
SEQ_UNROLL = 8
COL_Q, COL_K, COL_V, COL_O, COL_ZA, COL_U, COL_VB, COL_ZB, COL_GA, COL_GB = range(10)

NT_DIMS = (((1,), (1,)), ((), ()))

VMEM_LIMIT_BYTES = 56 * 1024 * 1024


def _sds(shape, dtype):
    return jax.ShapeDtypeStruct(shape, dtype)


def _iota(shape, dim):
    return lax.broadcasted_iota(jnp.int32, shape, dim)


def _dot(a, b):
    return jnp.dot(a, b, preferred_element_type=F32)


def _split_bf16(x):
    hi = x.astype(BF16)
    lo = (x - hi.astype(F32)).astype(BF16)
    return hi, lo


def _split3_bf16(x):
    hi = x.astype(BF16)
    rest = x - hi.astype(F32)
    mid = rest.astype(BF16)
    lo = (rest - mid.astype(F32)).astype(BF16)
    return hi, mid, lo


def _mask_dot_left(mask_b, x):
    hi, lo = _split_bf16(x)
    return _dot(mask_b, hi) + _dot(mask_b, lo)


def _mask_dot_right(x, mask_b):
    hi, lo = _split_bf16(x)
    return _dot(hi, mask_b) + _dot(lo, mask_b)


def _log_sigmoid(x):
    return -(jnp.maximum(-x, 0.0) + jnp.log1p(jnp.exp(-jnp.abs(x))))


def _sigmoid(x):
    return 0.5 * jnp.tanh(0.5 * x) + 0.5


def _silu(x):
    half = 0.5 * x
    return half + half * jnp.tanh(half)


def _segment_masks(seg_len):
    r = _iota((BLOCK_ROWS, BLOCK_ROWS), 0)
    c = _iota((BLOCK_ROWS, BLOCK_ROWS), 1)
    shift = seg_len.bit_length() - 1
    same = (r >> shift) == (c >> shift)
    causal = jnp.logical_and(same, c <= r)
    causal_t = jnp.logical_and(same, r <= c)
    return same, causal, causal_t


def _as_bf16_mask(m):
    return jnp.where(m, 1.0, 0.0).astype(BF16)


def _norm_kernel(xp_ref, xs_ref, g_ref, wg_ref, h_ref, gate_ref, *, n_prompt_blocks):
    def emit(x):
        ms = jnp.mean(x * x, axis=-1, keepdims=True)
        h = (x * lax.rsqrt(ms + EPS) * g_ref[...]).astype(BF16)
        h_ref[...] = h
        gate_ref[...] = lax.dot_general(h, wg_ref[...].astype(BF16), NT_DIMS,
                                        preferred_element_type=F32)

    i = pl.program_id(0)

    @pl.when(i < n_prompt_blocks)
    def _():
        emit(xp_ref[...])

    @pl.when(i >= n_prompt_blocks)
    def _():
        emit(xs_ref[...])


def _normalize(xp2d, xs2d, g_pre, w_in_t):
    tm = NORM_ROWS
    n_prompt_blocks = xp2d.shape[0] // tm
    nblk = n_prompt_blocks + xs2d.shape[0] // tm
    rows = nblk * tm
    return pl.pallas_call(
        functools.partial(_norm_kernel, n_prompt_blocks=n_prompt_blocks),
        grid=(nblk,),
        in_specs=[
            pl.BlockSpec((tm, D_MODEL), lambda i: (jnp.minimum(i, n_prompt_blocks - 1), 0)),
            pl.BlockSpec((tm, D_MODEL), lambda i: (jnp.maximum(i - n_prompt_blocks, 0), 0)),
            pl.BlockSpec((1, D_MODEL), lambda i: (0, 0)),
            pl.BlockSpec((LANES, D_MODEL), lambda i: (GATE_LO // LANES, 0)),
        ],
        out_specs=[
            pl.BlockSpec((tm, D_MODEL), lambda i: (i, 0)),
            pl.BlockSpec((tm, LANES), lambda i: (i, 0)),
        ],
        out_shape=[_sds((rows, D_MODEL), BF16), _sds((rows, LANES), F32)],
        compiler_params=pltpu.CompilerParams(
            dimension_semantics=("arbitrary",), vmem_limit_bytes=VMEM_LIMIT_BYTES),
        name="norm",
    )(xp2d, xs2d, g_pre, w_in_t)


def _cast_weight_block(wa_ref, wb_ref, w_scr, tn, shifted):
    def cast_rows(dst_lo, src_ref, src_lo, n):
        w_scr[dst_lo:dst_lo + n, :] = src_ref[src_lo:src_lo + n, :].astype(BF16)

    if not shifted:
        for lo in range(0, tn, W_CAST_ROWS):
            cast_rows(lo, wa_ref, lo, W_CAST_ROWS)
    else:
        for lo in range(0, tn - W_CAST_ROWS, W_CAST_ROWS):
            cast_rows(lo, wa_ref, lo + N_GATE_COLS, W_CAST_ROWS)
        cast_rows(tn - W_CAST_ROWS, wa_ref, tn - W_CAST_ROWS + N_GATE_COLS,
                  W_CAST_ROWS - N_GATE_COLS)
        cast_rows(tn - N_GATE_COLS, wb_ref, 0, N_GATE_COLS)


def _proj_kernel(h_ref, wa_ref, wb_ref, *refs, tn, shifted, n_side):
    side_in, o_ref, side_out, w_scr = (refs[:n_side], refs[n_side],
                                       refs[n_side + 1:2 * n_side + 1], refs[-1])

    @pl.when(pl.program_id(1) == 0)
    def _cast():
        _cast_weight_block(wa_ref, wb_ref, w_scr, tn, shifted)

    for src, dst in zip(side_in, side_out):
        dst[...] = src[...].astype(BF16)
    o_ref[...] = lax.dot_general(h_ref[...], w_scr[...], NT_DIMS,
                                 preferred_element_type=F32).astype(BF16)


def _proj_specs(tm, tn, col0):
    first = col0 // tn
    return ([pl.BlockSpec((tm, D_MODEL), lambda j, i: (i, 0)),
             pl.BlockSpec((tn, D_MODEL), lambda j, i: (first + j, 0)),
             pl.BlockSpec((N_GATE_COLS, D_MODEL),
                          lambda j, i: ((first + j + 1) * (tn // N_GATE_COLS), 0))],
            pl.BlockSpec((tm, tn), lambda j, i: (i, j)))


def _project(h_all, w_in_t, tm, tn, col0, ncols, side_weights=()):
    rows = h_all.shape[0]
    n_row_blocks = rows // tm
    in_specs, out_spec = _proj_specs(tm, tn, col0)
    n_slabs = D_MODEL // SIDE_CAST_ROWS
    assert n_slabs <= (ncols // tn) * n_row_blocks
    slab = pl.BlockSpec((SIDE_CAST_ROWS, D_MODEL),
                        lambda j, i: (jnp.minimum(j * n_row_blocks + i, n_slabs - 1), 0))
    n_side = len(side_weights)
    outs = pl.pallas_call(
        functools.partial(_proj_kernel, tn=tn, shifted=col0 >= GATE_LO, n_side=n_side),
        grid=(ncols // tn, n_row_blocks),
        in_specs=in_specs + [slab] * n_side,
        out_specs=[out_spec] + [slab] * n_side,
        out_shape=[_sds((rows, ncols), BF16)] + [_sds((D_MODEL, D_MODEL), BF16)] * n_side,
        scratch_shapes=[pltpu.VMEM((tn, D_MODEL), BF16)],
        compiler_params=pltpu.CompilerParams(
            dimension_semantics=("arbitrary", "arbitrary"),
            vmem_limit_bytes=VMEM_LIMIT_BYTES),
        name="proj",
    )(h_all, w_in_t, w_in_t, *side_weights)
    return outs[0], outs[1:]


def _gate_prep(graw, gbias, causal_b, causal_t_b):
    g = graw + gbias
    bcum = _mask_dot_left(causal_b, _log_sigmoid(g))
    gt = g.T
    bcum_t = _mask_dot_right(_log_sigmoid(gt), causal_t_b)
    return g, bcum, gt, bcum_t


def _conv_taps(x, w, bias, prev_fn):
    acc = bias + w[CONV_W - 1:CONV_W] * x
    for d in range(1, CONV_W):
        xd = prev_fn(d, pltpu.roll(x, d, axis=0))
        acc = acc + w[CONV_W - 1 - d:CONV_W - d] * xd
    return _silu(acc)


def _cell_output(s_qk, vh_b, bt, bs, ig_row, mprev_col, causal, num_inter, den_inter):
    dmat = jnp.where(causal, bt - bs + ig_row, -jnp.inf)
    inter = bt + mprev_col
    m_row = jnp.maximum(jnp.max(dmat, axis=1, keepdims=True), inter)
    wts = jnp.exp(dmat - m_row)
    sc = jnp.exp(inter - m_row)
    s = s_qk * wts
    num = _dot(s.astype(BF16), vh_b) + sc * num_inter
    den = jnp.sum(s, axis=1, keepdims=True) + sc * den_inter
    inv = 1.0 / jnp.maximum(jnp.abs(den), jnp.exp(-m_row))
    return num * inv


def _run_interleaved(stage_generators, side=None, side_period=1, finish_side=True):
    live = list(stage_generators)
    traced = 0
    while live:
        still = []
        for gen in live:
            if side is not None and traced % side_period == 0:
                next(side, None)
            traced += 1
            try:
                next(gen)
                still.append(gen)
            except StopIteration:
                pass
        live = still
    if side is not None and finish_side:
        for _ in side:
            pass


def _gated_head_out(hcell, ghead_row, o_b, za_b):
    hn = hcell * lax.rsqrt(jnp.mean(hcell * hcell, axis=-1, keepdims=True) + EPS) * ghead_row
    return (_sigmoid(o_b.astype(F32)) * hn * _silu(za_b.astype(F32))).astype(BF16)


def _prefix_max_lanes(x):
    lanes = x.shape[1]
    lane = _iota(x.shape, 1)
    k = 1
    while k < lanes:
        x = jnp.where(lane >= k, jnp.maximum(x, pltpu.roll(x, k, axis=1)), x)
        k *= 2
    return x


def _mlstm_chunk(qpre_ref, kpre_ref, qprev_ref, kprev_ref, v_ref, o_ref, za_ref, gate_ref,
                 gbias_ref, shift_ref, wconv_ref, bconv_ref, ghead_ref, ya_ref, c_ref, n_ref, m_ref,
                 side=None, side_period=1):
    ntap = CONV_W - 1
    k_scale = D_HEAD ** -0.5

    _, causal, causal_t = _segment_masks(BLOCK_ROWS)
    gt = (gate_ref[...] + gbias_ref[...]).T
    bcum_t = _mask_dot_right(_log_sigmoid(gt[N_HEADS:2 * N_HEADS, :]), _as_bf16_mask(causal_t))
    a_rows = gt[0:N_HEADS, :] - bcum_t
    per_row = jnp.concatenate(
        [bcum_t, _prefix_max_lanes(a_rows),
         jnp.zeros((BLOCK_ROWS - 2 * N_HEADS, BLOCK_ROWS), F32)], axis=0).T
    ones_2k = jnp.ones((2 * BLOCK_ROWS, LANES), BF16)
    shift = shift_ref[...]

    def both_halves(x):
        return jnp.concatenate([x, x], axis=1)

    def lane_replicated(x, lane):
        return jnp.broadcast_to(x[:, lane:lane + 1], (BLOCK_ROWS, LANES))

    def head_stages(h):
        cols = slice(h * D_HEAD, (h + 1) * D_HEAD)

        def shifted_rows(cur_ref, prev_ref):
            x_b = cur_ref[:, cols]
            return x_b, _dot(shift, jnp.concatenate([prev_ref[:, cols], x_b], axis=0))

        xq_b, sh_q = shifted_rows(qpre_ref, qprev_ref)
        xk_b, sh_k = shifted_rows(kpre_ref, kprev_ref)
        yield

        def conv_silu(x_b, shifted, off):
            ccols = slice(off + h * D_HEAD, off + (h + 1) * D_HEAD)
            w = 0.5 * wconv_ref[:, ccols]
            half = 0.5 * bconv_ref[:, ccols] + w[ntap:CONV_W] * x_b.astype(F32)
            for d in range(1, CONV_W):
                half = half + w[ntap - d:CONV_W - d] * shifted[(d - 1) * BLOCK_ROWS:d * BLOCK_ROWS]
            return (half + half * jnp.tanh(half)).astype(BF16)

        qb = conv_silu(xq_b, sh_q, 0)
        kb = conv_silu(xk_b, sh_k, D_MODEL)
        s_qk = lax.dot_general(qb, kb, NT_DIMS, preferred_element_type=F32)
        num_inter = lax.dot_general(qb, c_ref[h].astype(BF16), NT_DIMS, preferred_element_type=F32)
        yield

        mprev = m_ref[h:h + 1, 0:1]
        n_old = n_ref[h:h + 1, :]
        vh_b = v_ref[:, cols]
        bt = lane_replicated(per_row, h)
        c_row = jnp.maximum(lane_replicated(per_row, N_HEADS + h), mprev)
        a_row = a_rows[h:h + 1, :]
        s = s_qk * jnp.exp(jnp.where(causal, a_row + math.log(k_scale) - c_row, -jnp.inf))
        s_hi, s_lo = _split_bf16(s)
        num_intra = _dot(s_hi, vh_b)
        s_sum = _dot(jnp.concatenate([s_hi, s_lo], axis=1), ones_2k)
        q_dot_n = lax.dot_general(qb, jnp.broadcast_to(n_old.astype(BF16), (BLOCK_ROWS, D_HEAD)),
                                  NT_DIMS, preferred_element_type=F32)

        b_last = bcum_t[h:h + 1, BLOCK_ROWS - 1:BLOCK_ROWS]
        src_row = b_last + a_row
        m_new = jnp.maximum(b_last + mprev, jnp.max(src_row, axis=1, keepdims=True))
        ws_row = jnp.exp(src_row - m_new) * k_scale
        decay = jnp.exp(b_last + mprev - m_new)
        vw_t = (vh_b.T.astype(F32) * ws_row).astype(BF16)
        c_upd = _dot(vw_t, kb)
        n_upd = _dot(jnp.broadcast_to(ws_row.astype(BF16), (8, BLOCK_ROWS)), kb)[0:1, :]
        yield

        sc = jnp.exp(mprev - c_row)
        num = num_intra + both_halves(sc) * num_inter
        den = s_sum + sc * q_dot_n
        inv = 1.0 / jnp.maximum(jnp.abs(den), jnp.exp(-(bt + c_row)))
        msq = jnp.mean(num * num, axis=-1, keepdims=True)
        row_scale = inv * lax.rsqrt(inv * inv * msq + EPS)
        hn = num * both_halves(row_scale) * ghead_ref[:, cols]
        ya_ref[:, cols] = (_sigmoid(o_ref[:, cols].astype(F32)) * hn
                           * _silu(za_ref[:, cols].astype(F32))).astype(BF16)
        c_ref[h] = decay * c_ref[h] + c_upd
        n_ref[h:h + 1, :] = decay * n_old + n_upd
        m_ref[h:h + 1, :] = jnp.broadcast_to(m_new, (1, LANES))

    for h0 in range(0, N_HEADS, HEAD_GROUP):
        last = h0 + HEAD_GROUP == N_HEADS
        _run_interleaved([head_stages(h) for h in range(h0, h0 + HEAD_GROUP)], side, side_period,
                         finish_side=last)


def _conv_shift_matrices():
    ntap = CONV_W - 1
    r = jnp.arange(ntap * BLOCK_ROWS)[:, None]
    c = jnp.arange(2 * BLOCK_ROWS)[None, :]
    src = BLOCK_ROWS + (r % BLOCK_ROWS) - (r // BLOCK_ROWS + 1)
    later = c == src
    first = jnp.logical_and(later, c >= BLOCK_ROWS)
    return jnp.stack([first, later]).astype(BF16)


N_MLSTM_INPUTS = 13
PROJ_SLAB_COLS = 256
PROJ_ROW_SLABS = 1
MLSTM_HEAD_STAGES = 4


def _proj_mlstm_kernel(h_ref, wa_ref, wb_ref, *refs, tn, n_row_blocks, n_chunks, chunks_per_seq):
    mlstm_in = refs[:N_MLSTM_INPUTS]
    o_ref, ya_ref, c_ref, n_ref, m_ref, w_scr = refs[N_MLSTM_INPUTS:]
    step = pl.program_id(0) * n_row_blocks + pl.program_id(1)

    @pl.when(pl.program_id(1) == 0)
    def _cast():
        _cast_weight_block(wa_ref, wb_ref, w_scr, tn, shifted=True)

    tm = h_ref.shape[0]
    slab_rows = tm // PROJ_ROW_SLABS
    n_slabs = PROJ_ROW_SLABS * (tn // PROJ_SLAB_COLS)

    def project_slabs():
        for lo in range(0, tn, PROJ_SLAB_COLS):
            for r0 in range(0, tm, slab_rows):
                o_ref[r0:r0 + slab_rows, lo:lo + PROJ_SLAB_COLS] = lax.dot_general(
                    h_ref[r0:r0 + slab_rows, :], w_scr[lo:lo + PROJ_SLAB_COLS, :], NT_DIMS,
                    preferred_element_type=F32).astype(BF16)
                yield

    @pl.when(step < n_chunks)
    def _with_chunk():
        @pl.when(step % chunks_per_seq == 0)
        def _new_sequence():
            c_ref[...] = jnp.zeros_like(c_ref)
            n_ref[...] = jnp.zeros_like(n_ref)
            m_ref[...] = jnp.zeros_like(m_ref)

        head_stage_count = N_HEADS * MLSTM_HEAD_STAGES
        _mlstm_chunk(*mlstm_in, ya_ref, c_ref, n_ref, m_ref, side=project_slabs(),
                     side_period=max(1, head_stage_count // n_slabs))

    @pl.when(step >= n_chunks)
    def _projection_only():
        _run_interleaved([project_slabs()])


def _project_with_prompt_mlstm(h_all, w_in_t, proj_a, gates, gbias, w_conv, b_conv, g_head_row,
                               tm, tn, col0, ncols, batch, seq):
    rows = h_all.shape[0]
    n_row_blocks = rows // tm
    cps = seq // BLOCK_ROWS
    n_chunks = batch * cps
    assert n_chunks <= (ncols // tn) * n_row_blocks and col0 >= GATE_LO
    in_specs, out_spec = _proj_specs(tm, tn, col0)

    def chunk_of(j, i):
        return jnp.minimum(j * n_row_blocks + i, n_chunks - 1)

    def stream(kind):
        return pl.BlockSpec((BLOCK_ROWS, D_MODEL), lambda j, i: (chunk_of(j, i), kind))

    def prev_stream(kind):
        def index(j, i):
            chunk = chunk_of(j, i)
            return (chunk - jnp.minimum(chunk % cps, 1), kind)
        return pl.BlockSpec((BLOCK_ROWS, D_MODEL), index)

    def per_seq(shape):
        nd = len(shape)
        return pl.BlockSpec((None,) + shape, lambda j, i: (chunk_of(j, i) // cps,) + (0,) * nd)

    def const(shape):
        nd = len(shape)
        return pl.BlockSpec(shape, lambda j, i: (0,) * nd)

    shift_rows = (CONV_W - 1) * BLOCK_ROWS
    mlstm_specs = [
        stream(COL_Q), stream(COL_K), prev_stream(COL_Q), prev_stream(COL_K),
        stream(COL_V), stream(COL_O), stream(COL_ZA),
        pl.BlockSpec((BLOCK_ROWS, LANES), lambda j, i: (chunk_of(j, i), 0)),
        const((1, LANES)),
        pl.BlockSpec((None, shift_rows, 2 * BLOCK_ROWS),
                     lambda j, i: (jnp.minimum(chunk_of(j, i) % cps, 1), 0, 0)),
        const((CONV_W, 2 * D_MODEL)),
        const((1, 2 * D_MODEL)),
        const((1, D_MODEL)),
    ]
    assert len(mlstm_specs) == N_MLSTM_INPUTS
    return pl.pallas_call(
        functools.partial(_proj_mlstm_kernel, tn=tn, n_row_blocks=n_row_blocks, n_chunks=n_chunks,
                          chunks_per_seq=cps),
        grid=(ncols // tn, n_row_blocks),
        in_specs=in_specs + mlstm_specs,
        out_specs=[
            out_spec,
            pl.BlockSpec((BLOCK_ROWS, D_MODEL), lambda j, i: (chunk_of(j, i), 0)),
            per_seq((N_HEADS, D_HEAD, D_HEAD)),
            per_seq((N_HEADS, D_HEAD)),
            per_seq((N_HEADS, LANES)),
        ],
        out_shape=[
            _sds((rows, ncols), BF16),
            _sds((batch * seq, D_MODEL), BF16),
            _sds((batch, N_HEADS, D_HEAD, D_HEAD), F32),
            _sds((batch, N_HEADS, D_HEAD), F32),
            _sds((batch, N_HEADS, LANES), F32),
        ],
        scratch_shapes=[pltpu.VMEM((tn, D_MODEL), BF16)],
        compiler_params=pltpu.CompilerParams(
            dimension_semantics=("arbitrary", "arbitrary"),
            vmem_limit_bytes=VMEM_LIMIT_BYTES),
        name="proj_mlstm",
    )(h_all, w_in_t, w_in_t, proj_a, proj_a, proj_a, proj_a, proj_a, proj_a, proj_a, gates, gbias,
      _conv_shift_matrices(), w_conv, b_conv, g_head_row)


def _mlstm_decode_kernel(qpre_ref, kpre_ref, v_ref, o_ref, za_ref, gate_ref, gbias_ref,
                         bufq_ref, bufk_ref, c0_ref, n0_ref, m0_ref,
                         wq_ref, wk_ref, bq_ref, bk_ref, ghead_ref,
                         ya_ref, c_ref, n_ref, m_ref, rows_scr, dec_scr, *, seg_len):
    h = pl.program_id(1)
    nseg = BLOCK_ROWS // seg_len
    seg_shift = seg_len.bit_length() - 1
    same, causal, causal_t = _segment_masks(seg_len)
    same_b = _as_bf16_mask(same)
    g, bcum, gt, bcum_t = _gate_prep(gate_ref[...], gbias_ref[...],
                                     _as_bf16_mask(causal), _as_bf16_mask(causal_t))
    lf = _log_sigmoid(g)
    btot = _mask_dot_left(same_b, lf)
    btot_t = _mask_dot_right(_log_sigmoid(gt), same_b)
    rows_scr[0] = gt
    rows_scr[1] = bcum_t
    rows_scr[2] = btot_t
    ig_row = rows_scr[0, pl.ds(h, 1), :]
    bs = rows_scr[1, pl.ds(N_HEADS + h, 1), :]
    blast_row = rows_scr[2, pl.ds(N_HEADS + h, 1), :]

    lane = _iota((BLOCK_ROWS, LANES), 1)

    def pick_lane(x, idx):
        return jnp.sum(jnp.where(lane == idx, x, 0.0), axis=1, keepdims=True)

    ig_col = pick_lane(g, h)
    bt = pick_lane(bcum, N_HEADS + h)
    blast_col = pick_lane(btot, N_HEADS + h)
    m0_col = m0_ref[:, 0:1]

    t_idx = _iota((BLOCK_ROWS, D_HEAD), 0) & (seg_len - 1)

    expand = _as_bf16_mask((_iota((BLOCK_ROWS, nseg), 0) >> seg_shift) == _iota((BLOCK_ROWS, nseg), 1))

    def conv_head(pre_ref, buf_ref, w_ref, b_ref):
        def per_row(k):
            hi, mid, lo = _split3_bf16(buf_ref[k])
            return _dot(expand, hi) + _dot(expand, mid) + _dot(expand, lo)

        older = [per_row(k) for k in range(CONV_W - 1)]

        def prev_fn(d, xd):
            hd = older[CONV_W - 2]
            for t in range(d - 2, -1, -1):
                hd = jnp.where(t_idx == t, older[t + CONV_W - 1 - d], hd)
            return jnp.where(t_idx >= d, xd, hd)

        return _conv_taps(pre_ref[...].astype(F32), w_ref[...], b_ref[...], prev_fn)

    qh = conv_head(qpre_ref, bufq_ref, wq_ref, bq_ref)
    kh = conv_head(kpre_ref, bufk_ref, wk_ref, bk_ref) * (D_HEAD ** -0.5)
    qb = qh.astype(BF16)
    kb = kh.astype(BF16)
    vh_b = v_ref[...]

    src_col = blast_col - bt + ig_col
    src_row = blast_row - bs + ig_row
    seg_max = jnp.max(jnp.where(same, src_row, -jnp.inf), axis=1, keepdims=True)
    m_new_col = jnp.maximum(blast_col + m0_col, seg_max)
    ws_col = jnp.exp(src_col - m_new_col)
    decay_col = jnp.exp(blast_col + m0_col - m_new_col)
    dec_scr[...] = jnp.broadcast_to(decay_col, (BLOCK_ROWS, LANES))
    vwt = (vh_b.astype(F32) * ws_col).T

    row_seg = _iota((BLOCK_ROWS, D_HEAD), 0) >> seg_shift
    lane_seg = _iota((D_HEAD, BLOCK_ROWS), 1) >> seg_shift

    def per_sequence(b, acc):
        c_old = c0_ref[b, 0]
        q_b = jnp.where(row_seg == b, qh, 0.0).astype(BF16)
        acc = acc + lax.dot_general(q_b, c_old.astype(BF16), NT_DIMS, preferred_element_type=F32)
        vw_b = jnp.where(lane_seg == b, vwt, 0.0).astype(BF16)
        dec = dec_scr[pl.ds(b * seg_len, 1), 0:1]
        c_ref[b, 0] = dec * c_old + _dot(vw_b, kb)
        return acc

    num_inter = lax.fori_loop(0, nseg, per_sequence, jnp.zeros((BLOCK_ROWS, D_HEAD), F32),
                              unroll=SEQ_UNROLL)

    n_old = n0_ref[...]
    den_inter = jnp.sum(qh * n_old, axis=1, keepdims=True)
    s_qk = lax.dot_general(qb, kb, NT_DIMS, preferred_element_type=F32)
    hcell = _cell_output(s_qk, vh_b, bt, bs, ig_row, m0_col, causal, num_inter, den_inter)
    ya_ref[...] = _gated_head_out(hcell, ghead_ref[...], o_ref[...], za_ref[...])

    n_ref[...] = decay_col * n_old + _mask_dot_left(same_b, ws_col * kh)
    m_ref[...] = jnp.broadcast_to(m_new_col, (BLOCK_ROWS, LANES))


def _mlstm_decode(proj, gates, gbias, conv_buf_t, c0, n0_rows, m0_rows, w_conv, b_conv, g_head_row,
                  seg_len, row0):
    rows = conv_buf_t.shape[1] * seg_len
    nblk = rows // BLOCK_ROWS
    nseg = BLOCK_ROWS // seg_len
    blk0 = row0 // BLOCK_ROWS

    def conv_rows(off):
        return pl.BlockSpec((CONV_W - 1, nseg, D_HEAD), lambda i, h: (0, i, off * N_HEADS + h))

    def stream(kind):
        return pl.BlockSpec((BLOCK_ROWS, D_HEAD), lambda i, h: (blk0 + i, kind * N_HEADS + h))

    def head_cols(nrows, off):
        return pl.BlockSpec((nrows, D_HEAD), lambda i, h: (0, off * N_HEADS + h))

    row_head = pl.BlockSpec((BLOCK_ROWS, D_HEAD), lambda i, h: (i, h))
    state = pl.BlockSpec((nseg, 1, D_HEAD, D_HEAD), lambda i, h: (i, h, 0, 0))
    m_rows = pl.BlockSpec((None, BLOCK_ROWS, LANES), lambda i, h: (h, i, 0))

    return pl.pallas_call(
        functools.partial(_mlstm_decode_kernel, seg_len=seg_len),
        grid=(nblk, N_HEADS),
        in_specs=[
            stream(COL_Q), stream(COL_K), stream(COL_V), stream(COL_O), stream(COL_ZA),
            pl.BlockSpec((BLOCK_ROWS, LANES), lambda i, h: (blk0 + i, 0)),
            pl.BlockSpec((1, LANES), lambda i, h: (0, 0)),
            conv_rows(0), conv_rows(1),
            state, row_head, m_rows,
            head_cols(CONV_W, 0), head_cols(CONV_W, 1), head_cols(1, 0), head_cols(1, 1),
            head_cols(1, 0),
        ],
        out_specs=[row_head, state, row_head, m_rows],
        out_shape=[
            _sds((rows, D_MODEL), BF16),
            _sds(c0.shape, F32),
            _sds((rows, D_MODEL), F32),
            _sds((N_HEADS, rows, LANES), F32),
        ],
        scratch_shapes=[pltpu.VMEM((3, BLOCK_ROWS, BLOCK_ROWS), F32),
                        pltpu.VMEM((BLOCK_ROWS, LANES), F32)],
        compiler_params=pltpu.CompilerParams(
            dimension_semantics=("arbitrary", "arbitrary"),
            vmem_limit_bytes=VMEM_LIMIT_BYTES),
        name="mlstm_decode",
    )(proj, proj, proj, proj, proj, gates, gbias, conv_buf_t, conv_buf_t, c0, n0_rows, m0_rows,
      w_conv, w_conv, b_conv, b_conv, g_head_row)


def _merge_kernel(ya_ref, u_ref, vb_ref, zb_ref, ga_ref, gb_ref, x_ref, lng_ref, lnb_ref,
                  ws_ref, bcol_ref, wpa_ref, wpb_ref, wout_ref, gpost_ref,
                  y_ref, vrows_ref, yb_scr, *, seg_len, vrows_every, tm):
    _, causal, _ = _segment_masks(seg_len)
    vb = vb_ref[...].astype(F32)
    mu = jnp.mean(vb, axis=-1, keepdims=True)
    xc = vb - mu
    vbn = xc * lax.rsqrt(jnp.mean(xc * xc, axis=-1, keepdims=True) + EPS) * lng_ref[...] + lnb_ref[...]
    vbn_b = vbn.astype(BF16)

    gcols = D_MODEL // N_GROUPS
    for grp in range(N_GROUPS):
        w_g = jnp.where(causal, ws_ref[grp], 0.0).astype(BF16)
        bias = bcol_ref[:, grp:grp + 1]
        cols = slice(grp * gcols, (grp + 1) * gcols)
        for blk in range(tm // BLOCK_ROWS):
            rows = slice(blk * BLOCK_ROWS, (blk + 1) * BLOCK_ROWS)
            s = _dot(w_g, vbn_b[rows, cols]) + bias
            yb = u_ref[rows, cols].astype(F32) * s * _silu(zb_ref[rows, cols].astype(F32))
            yb_scr[rows, cols] = yb.astype(BF16)

    pa = _dot(ya_ref[...], wpa_ref[...])
    pb = _dot(yb_scr[...], wpb_ref[...])
    merged = (_sigmoid(ga_ref[...].astype(F32)) * pa
              + _sigmoid(gb_ref[...].astype(F32)) * pb)
    out = _dot(merged.astype(BF16), wout_ref[...])
    post = out * lax.rsqrt(jnp.mean(out * out, axis=-1, keepdims=True) + EPS) * gpost_ref[...]
    y_ref[...] = x_ref[...] + post

    if vrows_every == 1:
        vrows_ref[...] = vbn
    else:
        @pl.when(pl.program_id(0) % vrows_every == vrows_every - 1)
        def _():
            vrows_ref[...] = vbn[tm - BLOCK_ROWS:, :]


def _merge(ya2d, proj, x2d, ln_g, ln_b, w_eff, bcol, w_pa, w_pb, w_out, g_post,
           seg_len, rows_per_seq, tm, row0):
    rows = x2d.shape[0]
    blk0 = row0 // tm
    if rows_per_seq:
        vrows_every = rows_per_seq // tm
        vrows_shape = (rows // rows_per_seq * BLOCK_ROWS, D_MODEL)
        vrows_spec = pl.BlockSpec((BLOCK_ROWS, D_MODEL), lambda i: (i // vrows_every, 0))
    else:
        vrows_every = 1
        vrows_shape = (rows, D_MODEL)
        vrows_spec = pl.BlockSpec((tm, D_MODEL), lambda i: (i, 0))

    def stream(kind):
        return pl.BlockSpec((tm, D_MODEL), lambda i: (blk0 + i, kind - COL_U))

    def const(shape, single=False):
        nd = len(shape)
        if single:
            return pl.BlockSpec(shape, lambda i: (0,) * nd, pipeline_mode=pl.Buffered(1))
        return pl.BlockSpec(shape, lambda i: (0,) * nd)

    weight = const((D_MODEL, D_MODEL), single=True)
    return pl.pallas_call(
        functools.partial(_merge_kernel, seg_len=seg_len, vrows_every=vrows_every, tm=tm),
        grid=(rows // tm,),
        in_specs=[
            pl.BlockSpec((tm, D_MODEL), lambda i: (i, 0)),
            stream(COL_U), stream(COL_VB), stream(COL_ZB), stream(COL_GA), stream(COL_GB),
            pl.BlockSpec((tm, D_MODEL), lambda i: (i, 0)),
            const((1, D_MODEL)), const((1, D_MODEL)),
            const((N_GROUPS, BLOCK_ROWS, BLOCK_ROWS)), const((BLOCK_ROWS, LANES)),
            weight, weight, weight,
            const((1, D_MODEL)),
        ],
        out_specs=[pl.BlockSpec((tm, D_MODEL), lambda i: (i, 0)), vrows_spec],
        out_shape=[_sds((rows, D_MODEL), F32), _sds(vrows_shape, F32)],
        scratch_shapes=[pltpu.VMEM((tm, D_MODEL), BF16)],
        compiler_params=pltpu.CompilerParams(
            dimension_semantics=("arbitrary",),
            vmem_limit_bytes=VMEM_LIMIT_BYTES),
        name="merge",
    )(ya2d, proj, proj, proj, proj, proj, x2d, ln_g, ln_b, w_eff, bcol,
      w_pa, w_pb, w_out, g_post)


PROJ_COLS = 1024
PROJ_ROWS_MAX = 1280
BF16_SUBLANES = 16


def _proj_rows(rows):
    best = BF16_SUBLANES
    for tm in range(BF16_SUBLANES, PROJ_ROWS_MAX + 1, BF16_SUBLANES):
        if rows % tm == 0:
            best = tm
    return best


def _pad_lanes(a):
    return jnp.pad(a, ((0, 0), (0, LANES - a.shape[1])))


def kernel(x_prompt, x_sample, state_mlstm_C, state_mlstm_n, state_mlstm_m, state_conv_qk,
           g_pre, w_in, b_ig, b_fg, w_conv, b_conv, g_head, ln_g, ln_b, w_s, b_s,
           w_pa, w_pb, w_out, g_post):
    batch, seq, _ = x_prompt.shape
    dec_batch, dec_seq, _ = x_sample.shape
    dec_rows = dec_batch * dec_seq
    assert seq % BLOCK_ROWS == 0 and BLOCK_ROWS % dec_seq == 0 and dec_rows % BLOCK_ROWS == 0
    assert dec_seq >= CONV_W - 1 and dec_seq & (dec_seq - 1) == 0

    gbias = _pad_lanes(jnp.concatenate([b_ig, b_fg])[None, :])
    g_pre2 = g_pre[None, :]
    b_conv2 = b_conv[None, :]
    g_head_row = g_head.reshape(1, D_MODEL)
    ln_g2, ln_b2, g_post2 = ln_g[None, :], ln_b[None, :], g_post[None, :]

    p_rows = batch * seq
    xp2d = x_prompt.reshape(p_rows, D_MODEL)
    xs2d = x_sample.reshape(dec_rows, D_MODEL)
    assert p_rows % NORM_ROWS == 0 and dec_rows % NORM_ROWS == 0
    w_in_t = w_in.T
    h_all, gates = _normalize(xp2d, xs2d, g_pre2, w_in_t)
    tm = _proj_rows(p_rows + dec_rows)
    proj_a, (w_pa_b, w_pb_b, w_out_b) = _project(h_all, w_in_t, tm=tm, tn=PROJ_COLS, col0=0,
                                                 ncols=GATE_LO, side_weights=(w_pa, w_pb, w_out))
    proj_b, ya_p, c_p, n_p, m_pb = _project_with_prompt_mlstm(
        h_all, w_in_t, proj_a, gates, gbias, w_conv, b_conv2, g_head_row,
        tm=tm, tn=PROJ_COLS, col0=GATE_LO, ncols=N_MAIN - GATE_LO, batch=batch, seq=seq)

    y_p, vrows_p = _merge(ya_p, proj_b, xp2d, ln_g2, ln_b2,
                          w_s, _pad_lanes(b_s.T), w_pa_b, w_pb_b, w_out_b, g_post2,
                          seg_len=BLOCK_ROWS, rows_per_seq=seq, tm=256, row0=0)
    conv_p = jnp.stack([proj_a[(b + 1) * seq - (CONV_W - 1):(b + 1) * seq, :2 * D_MODEL]
                        for b in range(batch)]).astype(F32)

    proj_s = proj_a[p_rows:, :2 * D_MODEL]
    conv_buf_t = jnp.transpose(state_conv_qk, (1, 0, 2))
    n0_rows = jnp.repeat(state_mlstm_n.reshape(dec_batch, D_MODEL), dec_seq, axis=0)
    m0_rows = jnp.broadcast_to(jnp.repeat(state_mlstm_m.T, dec_seq, axis=1)[:, :, None],
                               (N_HEADS, dec_rows, LANES))
    ya_s, c_s, n_rows, m_rows = _mlstm_decode(
        proj_a, gates, gbias, conv_buf_t, state_mlstm_C, n0_rows, m0_rows, w_conv, b_conv2, g_head_row,
        seg_len=dec_seq, row0=p_rows)
    nrep = BLOCK_ROWS // dec_seq
    w_eff_s = jnp.tile(w_s[:, :dec_seq, :dec_seq], (1, nrep, nrep))
    bcol_s = _pad_lanes(jnp.tile(b_s[:, :dec_seq], (1, nrep)).T)
    y_s, vrows_s = _merge(ya_s, proj_b, xs2d, ln_g2, ln_b2, w_eff_s, bcol_s,
                          w_pa_b, w_pb_b, w_out_b, g_post2,
                          seg_len=dec_seq, rows_per_seq=0, tm=256, row0=p_rows)
    n_s = n_rows[::dec_seq].reshape(dec_batch, N_HEADS, D_HEAD)
    m_s = m_rows[:, ::dec_seq, 0].T
    conv_s = proj_s.reshape(dec_batch, dec_seq, 2 * D_MODEL)
    conv_s = conv_s[:, dec_seq - (CONV_W - 1):].astype(F32)

    return (y_p.reshape(batch, seq, D_MODEL), y_s.reshape(dec_batch, dec_seq, D_MODEL),
            c_p, n_p, m_pb[:, :, 0], conv_p, vrows_p.reshape(batch, BLOCK_ROWS, D_MODEL),
            c_s, n_s, m_s, conv_s, vrows_s.reshape(dec_batch, dec_seq, D_MODEL))
```
